```python
import math
import jax
import jax.numpy as jnp
from jax import lax
import numpy as np

D_MODEL = 1024
BATCH = 2
SEQ = 16384
DEPTH = 2
DEC_BATCH = 1
DEC_SEQ = 16384
PAST_LEN = 128

GRID_W = 64
Q_BLOCK = 128
NORM_EPS = 1e-6
ROPE_THETA = 10000.0
NEG_BIG = -1e30
A_HEADS = 8
A_NOPE = 64
A_ROPE = 32
A_V = 64
A_Q_LORA = 384
A_KV_LORA = 256
B_PAIRS = ((128, 1), (512, 4), (2048, 16))
B_GROUPS = 3
B_HEADS = 8
B_HD = 64
C_HEADS = 8
C_KV_HEADS = 2
C_HD = 64
NUM_BUCKETS = 32
MAX_DISTANCE = 2048
D_FF = 4 * D_MODEL
N_BRANCH = 3
A_WIDTH = A_HEADS * A_V
B_WIDTH = B_HEADS * B_HD
C_WIDTH = C_HEADS * C_HD
IN_SPLIT = (A_Q_LORA, A_KV_LORA, A_ROPE, 3 * B_GROUPS * B_HEADS * B_HD, C_HEADS * C_HD, C_KV_HEADS * C_HD, C_KV_HEADS * C_HD, N_BRANCH * D_MODEL)
IN_COLS = A_Q_LORA + A_KV_LORA + A_ROPE + 3 * B_GROUPS * B_HEADS * B_HD + C_HEADS * C_HD + 2 * C_KV_HEADS * C_HD + N_BRANCH * D_MODEL

kernel_name = "hybrid_mla_dilated_axialgqa_encoder"


def rms_norm(x, g):
    xf = x.astype(jnp.float32)
    y = xf * lax.rsqrt(jnp.mean(xf * xf, axis=-1, keepdims=True) + NORM_EPS)
    return (y * g.astype(jnp.float32)).astype(x.dtype)


def split_columns(z):
    parts = []
    start = 0
    for width in IN_SPLIT:
        parts.append(z[..., start:start + width])
        start += width
    return parts


def apply_rope(x, pos):
    half = x.shape[-1] // 2
    freqs = ROPE_THETA ** (-jnp.arange(half, dtype=jnp.float32) / half)
    ang = pos.astype(jnp.float32)[:, None] * freqs[None, :]
    cos = jnp.cos(ang)[None, :, None, :]
    sin = jnp.sin(ang)[None, :, None, :]
    xf = x.astype(jnp.float32)
    x1, x2 = xf[..., :half], xf[..., half:]
    return jnp.concatenate([x1 * cos - x2 * sin, x1 * sin + x2 * cos], axis=-1).astype(x.dtype)


def t5_bucket(rel):
    half = NUM_BUCKETS // 2
    max_exact = half // 2
    ret = jnp.where(rel > 0, half, 0)
    n = jnp.abs(rel)
    nf = jnp.maximum(n, 1).astype(jnp.float32)
    large = max_exact + (jnp.log(nf / max_exact) / math.log(MAX_DISTANCE / max_exact) * (half - max_exact)).astype(jnp.int32)
    large = jnp.minimum(large, half - 1)
    return ret + jnp.where(n < max_exact, n, large)


def blocked_attention(q, k, v):
    bsz, seq, hq, dq = q.shape
    hk = k.shape[2]
    dv = v.shape[-1]
    rep = hq // hk
    scale = dq ** -0.5
    qb = q.reshape(bsz, seq // Q_BLOCK, Q_BLOCK, hk, rep, dq).swapaxes(0, 1)

    def one_block(q_blk):
        s = jnp.einsum('bqgrd,bkgd->bgrqk', q_blk, k).astype(jnp.float32) * scale
        p = jax.nn.softmax(s, axis=-1)
        return jnp.einsum('bgrqk,bkgd->bqgrd', p.astype(v.dtype), v)

    o = lax.map(one_block, qb)
    return o.swapaxes(0, 1).reshape(bsz, seq, hq, dv)


def dilated_group_attention(q, k, v, bias_hj, dilation, n_side):
    bsz, seq, nh, hd = q.shape
    scale = hd ** -0.5
    offs = dilation * jnp.arange(-n_side, n_side + 1, dtype=jnp.int32)
    nblk = seq // Q_BLOCK
    qb = q.reshape(bsz, nblk, Q_BLOCK, nh, hd).swapaxes(0, 1)

    def one_block(args):
        q_blk, blk = args
        t = blk * Q_BLOCK + jnp.arange(Q_BLOCK, dtype=jnp.int32)
        idx = t[:, None] + offs[None, :]
        valid = (idx >= 0) & (idx < seq)
        idx_c = jnp.clip(idx, 0, seq - 1)
        k_g = k[:, idx_c]
        v_g = v[:, idx_c]
        s = jnp.einsum('bqhd,bqjhd->bhqj', q_blk, k_g).astype(jnp.float32) * scale + bias_hj[None, :, None, :]
        s = jnp.where(valid[None, None], s, NEG_BIG)
        lse = jax.nn.logsumexp(s, axis=-1)
        p = jnp.exp(s - lse[..., None])
        o = jnp.einsum('bhqj,bqjhd->bqhd', p.astype(v.dtype), v_g)
        return o, lse.transpose(0, 2, 1)

    o, lse = lax.map(one_block, (qb, jnp.arange(nblk, dtype=jnp.int32)))
    o = o.swapaxes(0, 1).reshape(bsz, seq, nh, hd)
    lse = lse.swapaxes(0, 1).reshape(bsz, seq, nh)
    return o, lse


def encoder_layer(x, norm_mix, w_in, a_q_norm, a_kv_norm, a_w_uq, a_w_ukv, c_q_norm, c_k_norm,
                  w_br_a, w_br_b, w_br_c, w_out, norm_ffn, w_up, w_down, t5_table):
    bsz, seq, _ = x.shape
    rows = seq // GRID_W
    pos = jnp.arange(seq, dtype=jnp.int32)
    row_idx = jnp.repeat(jnp.arange(rows, dtype=jnp.int32), GRID_W)
    col_idx = jnp.tile(jnp.arange(GRID_W, dtype=jnp.int32), rows)

    h = rms_norm(x, norm_mix)
    z = h @ w_in
    cq, ckv, kr, qkv_b, q_c, k_c, v_c, gate_logits = split_columns(z)

    cq = rms_norm(cq, a_q_norm)
    q_a = (cq @ a_w_uq).reshape(bsz, seq, A_HEADS, A_NOPE + A_ROPE)
    q_a = jnp.concatenate([q_a[..., :A_NOPE], apply_rope(q_a[..., A_NOPE:], pos)], axis=-1)
    ckv = rms_norm(ckv, a_kv_norm)
    kv_a = (ckv @ a_w_ukv).reshape(bsz, seq, A_HEADS, A_NOPE + A_V)
    k_rope = jnp.broadcast_to(apply_rope(kr[:, :, None, :], pos), (bsz, seq, A_HEADS, A_ROPE))
    k_a = jnp.concatenate([kv_a[..., :A_NOPE], k_rope], axis=-1)
    o_a = blocked_attention(q_a, k_a, kv_a[..., A_NOPE:]).reshape(bsz, seq, A_WIDTH)

    qkv_b = qkv_b.reshape(bsz, seq, 3, B_GROUPS, B_HEADS, B_HD)
    outs, lses = [], []
    for g, (window, dilation) in enumerate(B_PAIRS):
        n_side = window // (2 * dilation)
        offs = dilation * jnp.arange(-n_side, n_side + 1, dtype=jnp.int32)
        bias_hj = t5_table[t5_bucket(offs)][:, g * B_HEADS:(g + 1) * B_HEADS].T.astype(jnp.float32)
        o_g, lse_g = dilated_group_attention(qkv_b[:, :, 0, g], qkv_b[:, :, 1, g], qkv_b[:, :, 2, g], bias_hj, dilation, n_side)
        outs.append(o_g)
        lses.append(lse_g)
    wts = jax.nn.softmax(jnp.stack(lses, axis=0), axis=0)
    o_b = jnp.einsum('gbsh,gbshd->bshd', wts.astype(x.dtype), jnp.stack(outs, axis=0)).reshape(bsz, seq, B_WIDTH)

    q_c = rms_norm(q_c.reshape(bsz, seq, C_HEADS, C_HD), c_q_norm)
    k_c = rms_norm(k_c.reshape(bsz, seq, C_KV_HEADS, C_HD), c_k_norm)
    v_c = v_c.reshape(bsz, seq, C_KV_HEADS, C_HD)
    half = C_HD // 2
    q_c = jnp.concatenate([apply_rope(q_c[..., :half], row_idx), apply_rope(q_c[..., half:], col_idx)], axis=-1)
    k_c = jnp.concatenate([apply_rope(k_c[..., :half], row_idx), apply_rope(k_c[..., half:], col_idx)], axis=-1)
    o_c = blocked_attention(q_c, k_c, v_c).reshape(bsz, seq, C_WIDTH)

    gates = jax.nn.sigmoid(gate_logits.reshape(bsz, seq, N_BRANCH, D_MODEL))
    mix = gates[:, :, 0] * (o_a @ w_br_a) + gates[:, :, 1] * (o_b @ w_br_b) + gates[:, :, 2] * (o_c @ w_br_c)
    x = x + mix @ w_out

    h2 = rms_norm(x, norm_ffn)
    x = x + jnp.square(jax.nn.relu(h2 @ w_up)) @ w_down
    return x


def trunk(x, norm_mix, w_in, a_q_norm, a_kv_norm, a_w_uq, a_w_ukv, c_q_norm, c_k_norm,
          w_br_a, w_br_b, w_br_c, w_out, norm_ffn, w_up, w_down, t5_table, final_norm):
    for l in range(DEPTH):
        x = encoder_layer(x, norm_mix[l], w_in[l], a_q_norm[l], a_kv_norm[l], a_w_uq[l], a_w_ukv[l],
                          c_q_norm[l], c_k_norm[l], w_br_a[l], w_br_b[l], w_br_c[l], w_out[l],
                          norm_ffn[l], w_up[l], w_down[l], t5_table)
    return rms_norm(x, final_norm)


def setup_inputs(seed: int = 0) -> dict:
    key = jax.random.key(seed)
    ks = jax.random.split(key, 20)
    f32 = jnp.float32

    def dense(k, fan_in, fan_out):
        return jax.random.normal(k, (DEPTH, fan_in, fan_out), f32) * fan_in ** -0.5

    def gain(k, shape):
        return 1.0 + 0.01 * jax.random.normal(k, shape, f32)

    return {
        "x_prompt": jax.random.normal(ks[0], (BATCH, SEQ, D_MODEL), f32),
        "x_sample": jax.random.normal(ks[1], (DEC_BATCH, DEC_SEQ, D_MODEL), f32),
        "norm_mix": gain(ks[2], (DEPTH, D_MODEL)),
        "w_in": dense(ks[3], D_MODEL, IN_COLS),
        "a_q_norm": gain(ks[4], (DEPTH, A_Q_LORA)),
        "a_kv_norm": gain(ks[5], (DEPTH, A_KV_LORA)),
        "a_w_uq": dense(ks[6], A_Q_LORA, A_HEADS * (A_NOPE + A_ROPE)),
        "a_w_ukv": dense(ks[7], A_KV_LORA, A_HEADS * (A_NOPE + A_V)),
        "c_q_norm": gain(ks[8], (DEPTH, C_HD)),
        "c_k_norm": gain(ks[9], (DEPTH, C_HD)),
        "w_br_a": dense(ks[10], A_WIDTH, D_MODEL),
        "w_br_b": dense(ks[11], B_WIDTH, D_MODEL),
        "w_br_c": dense(ks[12], C_WIDTH, D_MODEL),
        "w_out": dense(ks[13], D_MODEL, D_MODEL),
        "norm_ffn": gain(ks[14], (DEPTH, D_MODEL)),
        "w_up": dense(ks[15], D_MODEL, D_FF),
        "w_down": dense(ks[16], D_FF, D_MODEL),
        "t5_table": 0.1 * jax.random.normal(ks[17], (NUM_BUCKETS, B_GROUPS * B_HEADS), f32),
        "final_norm": gain(ks[18], (D_MODEL,)),
    }


def reference(x_prompt, x_sample, norm_mix, w_in, a_q_norm, a_kv_norm, a_w_uq, a_w_ukv, c_q_norm, c_k_norm,
              w_br_a, w_br_b, w_br_c, w_out, norm_ffn, w_up, w_down, t5_table, final_norm):
    y_prompt = trunk(x_prompt, norm_mix, w_in, a_q_norm, a_kv_norm, a_w_uq, a_w_ukv, c_q_norm, c_k_norm,
                     w_br_a, w_br_b, w_br_c, w_out, norm_ffn, w_up, w_down, t5_table, final_norm)
    y_sample = trunk(x_sample, norm_mix, w_in, a_q_norm, a_kv_norm, a_w_uq, a_w_ukv, c_q_norm, c_k_norm,
                     w_br_a, w_br_b, w_br_c, w_out, norm_ffn, w_up, w_down, t5_table, final_norm)
    return (y_prompt, y_sample)
```

```python
import functools
import math

import jax
import jax.numpy as jnp
import numpy as np
from jax import lax
from jax.experimental import pallas as pl
from jax.experimental.pallas import tpu as pltpu

D_MODEL = 1024
DEPTH = 2
GRID_W = 64
NORM_EPS = 1e-6
ROPE_THETA = 10000.0
NEG_BIG = -1e30
A_HEADS, A_NOPE, A_ROPE, A_V = 8, 64, 32, 64
A_Q_LORA, A_KV_LORA = 384, 256
B_PAIRS = ((128, 1), (512, 4), (2048, 16))
B_GROUPS, B_HEADS, B_HD = 3, 8, 64
C_HEADS, C_KV_HEADS, C_HD = 8, 2, 64
NUM_BUCKETS, MAX_DISTANCE = 32, 2048
D_FF = 4 * D_MODEL
N_BRANCH = 3
B_SIDE = 64

LANES = 128
V_ROWS = 80
VMEM_LIMIT = 48 * 1024 * 1024

BF16 = jnp.bfloat16
F32 = jnp.float32


def _params(*sem):
    return pltpu.CompilerParams(dimension_semantics=sem, vmem_limit_bytes=VMEM_LIMIT)


def _rms(x, g):
    return x * lax.rsqrt(jnp.mean(x * x, axis=-1, keepdims=True) + NORM_EPS) * g


_C_CQ = (0, 384)
_C_CKV = (384, 640)
_C_KR = (640, 768)
_C_KRR = (768, 896)
_C_QC = (896, 1920)
_C_QCR = (1920, 2944)
_C_KC = (2944, 3200)
_C_KCR = (3200, 3456)
_C_VC = (3456, 3712)
_W1_COLS = 3712


def _prep_ac_body(x_ref, gmix_ref, w1_ref, gq_ref, gkv_ref, wq2_ref, wkv2_ref, cg_ref, tab_ref,
                  qa_ref, ka_ref, va_ref, qc_ref, kc_ref, vc_ref):
    h = _rms(x_ref[...], gmix_ref[...]).astype(BF16)
    z = jnp.dot(h, w1_ref[...], preferred_element_type=F32)
    cqn = _rms(z[:, _C_CQ[0]:_C_CQ[1]], gq_ref[...]).astype(BF16)
    ckvn = _rms(z[:, _C_CKV[0]:_C_CKV[1]], gkv_ref[...]).astype(BF16)
    qq = jnp.dot(cqn, wq2_ref[...], preferred_element_type=F32)
    kv = jnp.dot(ckvn, wkv2_ref[...], preferred_element_type=F32)
    cos_q, sin_q, cos_k, sin_k, cos_c, sin_c = (tab_ref[t] for t in range(6))
    krope = z[:, _C_KR[0]:_C_KR[1]] * cos_k + z[:, _C_KRR[0]:_C_KRR[1]] * sin_k
    lane = lax.broadcasted_iota(jnp.int32, (1, LANES), 1)
    ones_col = (lane == A_V).astype(F32)
    hw = A_HEADS * LANES
    for hd in range(A_HEADS):
        sl = slice(hd * LANES, (hd + 1) * LANES)
        sr = slice(hw + hd * LANES, hw + (hd + 1) * LANES)
        qa_ref[:, sl] = (qq[:, sl] * cos_q + qq[:, sr] * sin_q).astype(BF16)
        ka_ref[:, sl] = (kv[:, sl] + krope).astype(BF16)
        va_ref[:, sl] = (kv[:, sr] + ones_col).astype(BF16)
    gq_cos = cg_ref[0:1, :] * cos_c
    gq_sin = cg_ref[1:2, :] * sin_c
    gk_cos = cg_ref[2:3, :] * cos_c
    gk_sin = cg_ref[3:4, :] * sin_c
    for hd in range(C_HEADS):
        sl = slice(hd * LANES, (hd + 1) * LANES)
        q = z[:, _C_QC[0] + hd * LANES:_C_QC[0] + (hd + 1) * LANES]
        qr = z[:, _C_QCR[0] + hd * LANES:_C_QCR[0] + (hd + 1) * LANES]
        inv = lax.rsqrt(jnp.sum(q * q, axis=-1, keepdims=True) * (1.0 / C_HD) + NORM_EPS)
        qc_ref[:, sl] = ((q * gq_cos + qr * gq_sin) * inv).astype(BF16)
    for hd in range(C_KV_HEADS):
        sl = slice(hd * LANES, (hd + 1) * LANES)
        k = z[:, _C_KC[0] + hd * LANES:_C_KC[0] + (hd + 1) * LANES]
        kr = z[:, _C_KCR[0] + hd * LANES:_C_KCR[0] + (hd + 1) * LANES]
        inv = lax.rsqrt(jnp.sum(k * k, axis=-1, keepdims=True) * (1.0 / C_HD) + NORM_EPS)
        kc_ref[:, sl] = ((k * gk_cos + kr * gk_sin) * inv).astype(BF16)
        vc_ref[:, sl] = (z[:, _C_VC[0] + hd * LANES:_C_VC[0] + (hd + 1) * LANES] + ones_col).astype(BF16)


def _prep_ac(x, gmix, w1, gq, gkv, wq2, wkv2, cg, tabs, *, seq, tm):
    rows = x.shape[0]
    tiles_per_seq = seq // tm
    full = lambda a: pl.BlockSpec(a.shape, lambda i: (0,) * a.ndim)
    row_spec = lambda w: pl.BlockSpec((tm, w), lambda i: (i, 0))
    wa, wc, wk = A_HEADS * LANES, C_HEADS * LANES, C_KV_HEADS * LANES
    return pl.pallas_call(
        _prep_ac_body,
        grid=(rows // tm,),
        in_specs=[row_spec(D_MODEL), full(gmix), full(w1), full(gq), full(gkv), full(wq2), full(wkv2), full(cg),
                  pl.BlockSpec((6, tm, LANES), lambda i: (0, i % tiles_per_seq, 0))],
        out_specs=[row_spec(wa), row_spec(wa), row_spec(wa), row_spec(wc), row_spec(wk), row_spec(wk)],
        out_shape=[jax.ShapeDtypeStruct((rows, w), BF16) for w in (wa, wa, wa, wc, wk, wk)],
        compiler_params=_params("arbitrary"),
        name="prep_ac",
    )(x, gmix, w1, gq, gkv, wq2, wkv2, cg, tabs)


def _rms_matmul_body(x_ref, g_ref, w_ref, o_ref):
    h = _rms(x_ref[...], g_ref[...]).astype(BF16)
    o_ref[...] = jnp.dot(h, w_ref[...], preferred_element_type=F32).astype(o_ref.dtype)


def _rms_matmul(x, g, w, out_dtype, *, tm, name):
    rows = x.shape[0]
    n = w.shape[1]
    return pl.pallas_call(
        _rms_matmul_body,
        grid=(rows // tm,),
        in_specs=[pl.BlockSpec((tm, D_MODEL), lambda i: (i, 0)),
                  pl.BlockSpec(g.shape, lambda i: (0, 0)),
                  pl.BlockSpec(w.shape, lambda i: (0, 0))],
        out_specs=pl.BlockSpec((tm, n), lambda i: (i, 0)),
        out_shape=jax.ShapeDtypeStruct((rows, n), out_dtype),
        compiler_params=_params("arbitrary"),
        name=name,
    )(x, g, w)


def _flash_body(qT_ref, k_ref, vT_ref, oT_ref, acc_ref, m_ref, *, tk, n_chunks, dv):
    qT = qT_ref[...]
    m_ref[...] = jnp.full(m_ref.shape, NEG_BIG, F32)
    acc_ref[...] = jnp.zeros(acc_ref.shape, F32)

    def chunk(c, carry):
        off = pl.multiple_of(c * tk, tk)
        sT = jnp.dot(k_ref[pl.ds(off, tk), :], qT, preferred_element_type=F32)
        m_prev = m_ref[...]
        m_new = jnp.maximum(m_prev, jnp.max(sT, axis=0, keepdims=True))
        p = jnp.exp(sT - m_new).astype(BF16)
        alpha = jnp.exp(m_prev - m_new)
        acc_ref[...] = acc_ref[...] * alpha + jnp.dot(vT_ref[c], p, preferred_element_type=F32)
        m_ref[...] = m_new
        return carry

    lax.fori_loop(0, n_chunks, chunk, 0)
    acc = acc_ref[...]
    oT_ref[...] = (acc[0:dv, :] * (1.0 / acc[dv:dv + 1, :])).astype(oT_ref.dtype)


def _flash(qT, k, vT, *, rep, tq, tk, name):
    nb, hq, _, seq = qT.shape
    n_chunks = seq // tk
    dv = A_V
    body = functools.partial(_flash_body, tk=tk, n_chunks=n_chunks, dv=dv)
    return pl.pallas_call(
        body,
        grid=(nb, hq, seq // tq),
        in_specs=[pl.BlockSpec((None, None, LANES, tq), lambda b, h, i: (b, h, 0, i)),
                  pl.BlockSpec((None, seq, LANES), lambda b, h, i: (b, 0, h // rep)),
                  pl.BlockSpec((None, None, n_chunks, V_ROWS, tk), lambda b, h, i: (b, h // rep, 0, 0, 0))],
        out_specs=pl.BlockSpec((None, None, dv, tq), lambda b, h, i: (b, h, 0, i)),
        out_shape=jax.ShapeDtypeStruct((nb, hq, dv, seq), BF16),
        scratch_shapes=[pltpu.VMEM((V_ROWS, tq), F32), pltpu.VMEM((1, tq), F32)],
        compiler_params=_params("arbitrary", "arbitrary", "arbitrary"),
        name=name,
    )(qT, k, vT)


B_TU = 128
B_WIN = B_TU + 2 * B_SIDE


def _dilated_body(q_ref, kp_ref, kc_ref, kn_ref, vp_ref, vc_ref, vn_ref, bias_ref, o_ref, lse_ref, *, n_u):
    u0 = pl.program_id(2) * B_TU
    kw = jnp.concatenate([kp_ref[B_TU - B_SIDE:, :], kc_ref[...], kn_ref[:B_SIDE, :]], axis=0)
    vw = jnp.concatenate([vp_ref[B_TU - B_SIDE:, :], vc_ref[...], vn_ref[:B_SIDE, :]], axis=0)
    key_u = u0 - B_SIDE + lax.broadcasted_iota(jnp.int32, (B_TU, B_WIN), 1)
    valid = (key_u >= 0) & (key_u < n_u)
    lane = lax.broadcasted_iota(jnp.int32, (B_TU, LANES), 1)
    low = lane < B_HD
    for hp in range(B_HEADS // 2):
        sl = slice(hp * LANES, (hp + 1) * LANES)
        qp = q_ref[:, sl]
        kp = kw[:, sl]
        vp = vw[:, sl]
        outs, lses = [], []
        for sub in range(2):
            qm = jnp.where(low if sub == 0 else jnp.logical_not(low), qp, jnp.zeros_like(qp))
            s = lax.dot_general(qm, kp, (((1,), (1,)), ((), ())), preferred_element_type=F32)
            s = jnp.where(valid, s + bias_ref[2 * hp + sub], NEG_BIG)
            m = jnp.max(s, axis=-1, keepdims=True)
            p = jnp.exp(s - m)
            l = jnp.sum(p, axis=-1, keepdims=True)
            o = jnp.dot(p.astype(BF16), vp, preferred_element_type=F32) * (1.0 / l)
            outs.append(o)
            lses.append(jnp.broadcast_to(m + jnp.log(l), (B_TU, LANES)))
        o_ref[:, sl] = jnp.where(low, outs[0], outs[1]).astype(o_ref.dtype)
        lse_ref[:, sl] = jnp.where(low, lses[0], lses[1])


def _dilated(zb, bias, *, group, dilation, seq, name):
    nb = zb.shape[0]
    n_u = seq // dilation
    nblk = n_u // B_TU
    hw = B_HEADS * B_HD
    ncol = 3 * B_GROUPS
    zv = zb.reshape(nb, n_u, dilation * ncol * hw)

    def spec(part, shift):
        def imap(b, r, i):
            blk = jnp.clip(i + shift, 0, nblk - 1)
            return (b, blk, r * ncol + 3 * group + part)
        return pl.BlockSpec((None, B_TU, hw), imap)

    out_spec = pl.BlockSpec((None, B_TU, hw), lambda b, r, i: (b, i, r))
    o, lse = pl.pallas_call(
        functools.partial(_dilated_body, n_u=n_u),
        grid=(nb, dilation, nblk),
        in_specs=[spec(0, 0), spec(1, -1), spec(1, 0), spec(1, 1), spec(2, -1), spec(2, 0), spec(2, 1),
                  pl.BlockSpec(bias.shape, lambda b, r, i: (0, 0, 0))],
        out_specs=[out_spec, out_spec],
        out_shape=[jax.ShapeDtypeStruct((nb, n_u, dilation * hw), BF16),
                   jax.ShapeDtypeStruct((nb, n_u, dilation * hw), F32)],
        compiler_params=_params("arbitrary", "arbitrary", "arbitrary"),
        name=name,
    )(zv, zv, zv, zv, zv, zv, zv, bias)
    return o.reshape(nb * seq, hw), lse.reshape(nb * seq, hw)


def _merge_body(x_ref, oa_ref, oc_ref, ob0_ref, ob1_ref, ob2_ref, l0_ref, l1_ref, l2_ref, gate_ref,
                wa_ref, wb_ref, wc_ref, wo_ref, out_ref):
    l0, l1, l2 = l0_ref[...], l1_ref[...], l2_ref[...]
    mx = jnp.maximum(jnp.maximum(l0, l1), l2)
    e0, e1, e2 = jnp.exp(l0 - mx), jnp.exp(l1 - mx), jnp.exp(l2 - mx)
    ob = (e0 * ob0_ref[...].astype(F32) + e1 * ob1_ref[...].astype(F32) + e2 * ob2_ref[...].astype(F32))
    ob = (ob * (1.0 / (e0 + e1 + e2))).astype(BF16)
    sg = 1.0 / (1.0 + jnp.exp(-gate_ref[...]))
    mix = sg[:, 0:D_MODEL] * jnp.dot(oa_ref[...], wa_ref[...], preferred_element_type=F32)
    mix += sg[:, D_MODEL:2 * D_MODEL] * jnp.dot(ob, wb_ref[...], preferred_element_type=F32)
    mix += sg[:, 2 * D_MODEL:] * jnp.dot(oc_ref[...], wc_ref[...], preferred_element_type=F32)
    out_ref[...] = x_ref[...] + jnp.dot(mix.astype(BF16), wo_ref[...], preferred_element_type=F32)


def _merge(x, oa, oc, obs, lses, gates, wa, wb, wc, wo, *, tm):
    rows = x.shape[0]
    row_spec = lambda w: pl.BlockSpec((tm, w), lambda i: (i, 0))
    full = lambda a: pl.BlockSpec(a.shape, lambda i: (0, 0))
    hw = B_HEADS * B_HD
    return pl.pallas_call(
        _merge_body,
        grid=(rows // tm,),
        in_specs=[row_spec(D_MODEL)] + [row_spec(hw)] * 8 + [row_spec(N_BRANCH * D_MODEL)]
                 + [full(wa), full(wb), full(wc), full(wo)],
        out_specs=row_spec(D_MODEL),
        out_shape=jax.ShapeDtypeStruct((rows, D_MODEL), F32),
        compiler_params=_params("arbitrary"),
        name="merge_out",
    )(x, oa, oc, *obs, *lses, gates, wa, wb, wc, wo)


FFN_CHUNK = 1024


def _ffn_body(x_ref, g_ref, wup_ref, wdn_ref, gfin_ref, out_ref, *, final):
    x = x_ref[...]
    h = _rms(x, g_ref[...]).astype(BF16)
    acc = x
    for c in range(D_FF // FFN_CHUNK):
        cs = slice(c * FFN_CHUNK, (c + 1) * FFN_CHUNK)
        u = jnp.dot(h, wup_ref[:, cs], preferred_element_type=F32)
        a = jnp.square(jnp.maximum(u, 0.0)).astype(BF16)
        acc = acc + jnp.dot(a, wdn_ref[cs, :], preferred_element_type=F32)
    if final:
        acc = _rms(acc, gfin_ref[...])
    out_ref[...] = acc


def _ffn(x, g, wup, wdn, gfin, *, final, tm):
    rows = x.shape[0]
    full = lambda a: pl.BlockSpec(a.shape, lambda i: (0, 0))
    return pl.pallas_call(
        functools.partial(_ffn_body, final=final),
        grid=(rows // tm,),
        in_specs=[pl.BlockSpec((tm, D_MODEL), lambda i: (i, 0)), full(g), full(wup), full(wdn), full(gfin)],
        out_specs=pl.BlockSpec((tm, D_MODEL), lambda i: (i, 0)),
        out_shape=jax.ShapeDtypeStruct((rows, D_MODEL), F32),
        compiler_params=_params("arbitrary"),
        name="ffn_final" if final else "ffn",
    )(x, g, wup, wdn, gfin)


def _pad_heads(w, n_heads, hd, lane_off=0):
    k = w.shape[0]
    w = w.reshape(k, n_heads, hd)
    w = jnp.pad(w, ((0, 0), (0, 0), (lane_off, LANES - lane_off - hd)))
    return w.reshape(k, n_heads * LANES)


def _rot_cols(w):
    half = w.shape[-1] // 2
    return jnp.concatenate([-w[..., half:], w[..., :half]], axis=-1)


def _rot_axial(w):
    half = C_HD // 2
    return jnp.concatenate([_rot_cols(w[..., :half]), _rot_cols(w[..., half:])], axis=-1)


def _swap_axial(g):
    q = C_HD // 4
    return jnp.concatenate([g[q:2 * q], g[:q], g[3 * q:], g[2 * q:3 * q]], axis=-1)


def _rope_angles(pos, half):
    freqs = ROPE_THETA ** (-jnp.arange(half, dtype=F32) / half)
    return pos.astype(F32)[:, None] * freqs[None, :]


def _tables(seq):
    pos = jnp.arange(seq, dtype=jnp.int32)
    ang = _rope_angles(pos, A_ROPE // 2)
    cos_r = jnp.tile(jnp.cos(ang), (1, 2))
    sin_r = jnp.tile(jnp.sin(ang), (1, 2))
    zeros = lambda w: jnp.zeros((seq, w), F32)
    cos_k = jnp.concatenate([zeros(A_NOPE), cos_r, zeros(LANES - A_NOPE - A_ROPE)], axis=1)
    sin_k = jnp.concatenate([zeros(A_NOPE), sin_r, zeros(LANES - A_NOPE - A_ROPE)], axis=1)
    scale_a = (A_NOPE + A_ROPE) ** -0.5
    nope = jnp.concatenate([jnp.ones((seq, A_NOPE), F32), zeros(LANES - A_NOPE)], axis=1)
    cos_q = (nope + cos_k) * scale_a
    sin_q = sin_k * scale_a
    quarter = C_HD // 4
    ang_r = _rope_angles(pos // GRID_W, quarter)
    ang_c = _rope_angles(pos % GRID_W, quarter)
    cos_c = jnp.concatenate([jnp.tile(jnp.cos(ang_r), (1, 2)), jnp.tile(jnp.cos(ang_c), (1, 2)), zeros(LANES - C_HD)], axis=1)
    sin_c = jnp.concatenate([jnp.tile(jnp.sin(ang_r), (1, 2)), jnp.tile(jnp.sin(ang_c), (1, 2)), zeros(LANES - C_HD)], axis=1)
    return jnp.stack([cos_q, sin_q, cos_k, sin_k, cos_c, sin_c], axis=0)


def _t5_bucket(rel):
    half = NUM_BUCKETS // 2
    max_exact = half // 2
    ret = jnp.where(rel > 0, half, 0)
    n = jnp.abs(rel)
    nf = jnp.maximum(n, 1).astype(F32)
    large = max_exact + (jnp.log(nf / max_exact) / math.log(MAX_DISTANCE / max_exact) * (half - max_exact)).astype(jnp.int32)
    large = jnp.minimum(large, half - 1)
    return ret + jnp.where(n < max_exact, n, large)


def _band_bias(t5_table, group, dilation):
    offs = dilation * jnp.arange(-B_SIDE, B_SIDE + 1, dtype=jnp.int32)
    bias_hj = t5_table[_t5_bucket(offs)][:, group * B_HEADS:(group + 1) * B_HEADS].T.astype(F32)
    rel = jnp.arange(B_WIN, dtype=jnp.int32)[None, :] - jnp.arange(B_TU, dtype=jnp.int32)[:, None]
    inside = (rel >= 0) & (rel <= 2 * B_SIDE)
    vals = bias_hj[:, jnp.clip(rel, 0, 2 * B_SIDE)]
    return jnp.where(inside[None], vals, NEG_BIG)


def _layer_weights(l, norm_mix, w_in, a_q_norm, a_kv_norm, a_w_uq, a_w_ukv, c_q_norm, c_k_norm,
                   w_br_a, w_br_b, w_br_c, w_out, norm_ffn, w_up, w_down):
    w = w_in[l]
    o = 0
    cols = []
    for width in (A_Q_LORA, A_KV_LORA, A_ROPE, 3 * B_GROUPS * B_HEADS * B_HD, C_HEADS * C_HD,
                  C_KV_HEADS * C_HD, C_KV_HEADS * C_HD, N_BRANCH * D_MODEL):
        cols.append(w[:, o:o + width])
        o += width
    w_cq, w_ckv, w_kr, w_b, w_qc, w_kc, w_vc, w_gate = cols
    kr_p = _pad_heads(w_kr, 1, A_ROPE, A_NOPE)
    kr_rot_p = _pad_heads(_rot_cols(w_kr), 1, A_ROPE, A_NOPE)
    qc = w_qc.reshape(D_MODEL, C_HEADS, C_HD)
    kc = w_kc.reshape(D_MODEL, C_KV_HEADS, C_HD)
    w1 = jnp.concatenate([
        w_cq, w_ckv, kr_p, kr_rot_p,
        _pad_heads(w_qc, C_HEADS, C_HD), _pad_heads(_rot_axial(qc).reshape(D_MODEL, -1), C_HEADS, C_HD),
        _pad_heads(w_kc, C_KV_HEADS, C_HD), _pad_heads(_rot_axial(kc).reshape(D_MODEL, -1), C_KV_HEADS, C_HD),
        _pad_heads(w_vc, C_KV_HEADS, C_HD)], axis=1).astype(BF16)
    uq = a_w_uq[l].reshape(A_Q_LORA, A_HEADS, A_NOPE + A_ROPE)
    uq_rot = jnp.concatenate([jnp.zeros((A_Q_LORA, A_HEADS, A_NOPE), F32), _rot_cols(uq[..., A_NOPE:])], axis=-1)
    wq2 = jnp.concatenate([_pad_heads(uq.reshape(A_Q_LORA, -1), A_HEADS, A_NOPE + A_ROPE),
                           _pad_heads(uq_rot.reshape(A_Q_LORA, -1), A_HEADS, A_NOPE + A_ROPE)], axis=1).astype(BF16)
    ukv = a_w_ukv[l].reshape(A_KV_LORA, A_HEADS, A_NOPE + A_V)
    wkv2 = jnp.concatenate([_pad_heads(ukv[..., :A_NOPE].reshape(A_KV_LORA, -1), A_HEADS, A_NOPE),
                            _pad_heads(ukv[..., A_NOPE:].reshape(A_KV_LORA, -1), A_HEADS, A_V)], axis=1).astype(BF16)
    pad_gain = lambda g: jnp.pad(g, (0, LANES - C_HD))
    scale_c = C_HD ** -0.5
    cg = jnp.stack([pad_gain(c_q_norm[l]) * scale_c, pad_gain(_swap_axial(c_q_norm[l])) * scale_c,
                    pad_gain(c_k_norm[l]), pad_gain(_swap_axial(c_k_norm[l]))], axis=0)
    wb5 = w_b.reshape(D_MODEL, 3, B_GROUPS, B_HEADS * B_HD)
    wb5 = wb5 * jnp.array([B_HD ** -0.5, 1.0, 1.0], F32)[None, :, None, None]
    w_bq = jnp.transpose(wb5, (0, 2, 1, 3)).reshape(D_MODEL, -1).astype(BF16)
    return dict(
        gmix=norm_mix[l][None, :], w1=w1, gq=a_q_norm[l][None, :], gkv=a_kv_norm[l][None, :], wq2=wq2, wkv2=wkv2,
        cg=cg, w_bq=w_bq, w_gate=w_gate.astype(BF16),
        wa=w_br_a[l].astype(BF16), wb=w_br_b[l].astype(BF16), wc=w_br_c[l].astype(BF16), wo=w_out[l].astype(BF16),
        gffn=norm_ffn[l][None, :], wup=w_up[l].astype(BF16), wdn=w_down[l].astype(BF16))


def _heads_to_lanes(a, nb, seq, n_heads):
    return jnp.transpose(a.reshape(nb, seq, n_heads, LANES), (0, 2, 3, 1))


def _value_chunks(v, nb, seq, n_heads, tk):
    vt = _heads_to_lanes(v, nb, seq, n_heads)[:, :, :V_ROWS, :]
    return jnp.transpose(vt.reshape(nb, n_heads, V_ROWS, seq // tk, tk), (0, 1, 3, 2, 4))


def _encoder_layer(x, wts, tabs, biases, *, nb, seq, final, final_norm, tm, tq, tk):
    qa, ka, va, qc, kc, vc = _prep_ac(x, wts["gmix"], wts["w1"], wts["gq"], wts["gkv"], wts["wq2"], wts["wkv2"],
                                      wts["cg"], tabs, seq=seq, tm=tm)
    zb = _rms_matmul(x, wts["gmix"], wts["w_bq"], BF16, tm=tm, name="proj_b")
    gates = _rms_matmul(x, wts["gmix"], wts["w_gate"], F32, tm=tm, name="proj_gate")

    oa_t = _flash(_heads_to_lanes(qa, nb, seq, A_HEADS), ka.reshape(nb, seq, -1),
                  _value_chunks(va, nb, seq, A_HEADS, tk), rep=1, tq=tq, tk=tk, name="flash_a")
    oc_t = _flash(_heads_to_lanes(qc, nb, seq, C_HEADS), kc.reshape(nb, seq, -1),
                  _value_chunks(vc, nb, seq, C_KV_HEADS, tk), rep=C_HEADS // C_KV_HEADS, tq=tq, tk=tk, name="flash_c")
    to_rows = lambda o_t: jnp.transpose(o_t, (0, 3, 1, 2)).reshape(nb * seq, -1)
    oa, oc = to_rows(oa_t), to_rows(oc_t)

    zb3 = zb.reshape(nb, seq, -1)
    obs, lses = [], []
    for g, (_, dilation) in enumerate(B_PAIRS):
        o_g, lse_g = _dilated(zb3, biases[g], group=g, dilation=dilation, seq=seq, name=f"dilated_{g}")
        obs.append(o_g)
        lses.append(lse_g)

    x = _merge(x, oa, oc, obs, lses, gates, wts["wa"], wts["wb"], wts["wc"], wts["wo"], tm=tm)
    return _ffn(x, wts["gffn"], wts["wup"], wts["wdn"], final_norm, final=final, tm=tm)


def _trunk(x, norm_mix, w_in, a_q_norm, a_kv_norm, a_w_uq, a_w_ukv, c_q_norm, c_k_norm,
           w_br_a, w_br_b, w_br_c, w_out, norm_ffn, w_up, w_down, t5_table, final_norm, *, tm, tq, tk):
    nb, seq, _ = x.shape
    tabs = _tables(seq)
    biases = [_band_bias(t5_table, g, d) for g, (_, d) in enumerate(B_PAIRS)]
    depth = w_in.shape[0]
    xr = x.reshape(nb * seq, D_MODEL)
    for l in range(depth):
        wts = _layer_weights(l, norm_mix, w_in, a_q_norm, a_kv_norm, a_w_uq, a_w_ukv, c_q_norm, c_k_norm,
                             w_br_a, w_br_b, w_br_c, w_out, norm_ffn, w_up, w_down)
        xr = _encoder_layer(xr, wts, tabs, biases, nb=nb, seq=seq, final=(l == depth - 1),
                            final_norm=final_norm[None, :], tm=tm, tq=tq, tk=tk)
    return xr.reshape(nb, seq, D_MODEL)


def kernel(x_prompt, x_sample, norm_mix, w_in, a_q_norm, a_kv_norm, a_w_uq, a_w_ukv, c_q_norm, c_k_norm,
           w_br_a, w_br_b, w_br_c, w_out, norm_ffn, w_up, w_down, t5_table, final_norm):
    assert x_prompt.shape[1:] == x_sample.shape[1:]
    n_prompt = x_prompt.shape[0]
    x = jnp.concatenate([x_prompt, x_sample], axis=0)
    y = _trunk(x, norm_mix, w_in, a_q_norm, a_kv_norm, a_w_uq, a_w_ukv, c_q_norm, c_k_norm,
               w_br_a, w_br_b, w_br_c, w_out, norm_ffn, w_up, w_down, t5_table, final_norm,
               tm=256, tq=1024, tk=512)
    return (y[:n_prompt], y[n_prompt:])
```

```python
import functools
import math

import jax
import jax.numpy as jnp
import numpy as np
from jax import lax
from jax.experimental import pallas as pl
from jax.experimental.pallas import tpu as pltpu

D_MODEL = 1024
DEPTH = 2
GRID_W = 64
NORM_EPS = 1e-6
ROPE_THETA = 10000.0
NEG_BIG = -1e30
A_HEADS, A_NOPE, A_ROPE, A_V = 8, 64, 32, 64
A_Q_LORA, A_KV_LORA = 384, 256
B_PAIRS = ((128, 1), (512, 4), (2048, 16))
B_GROUPS, B_HEADS, B_HD = 3, 8, 64
C_HEADS, C_KV_HEADS, C_HD = 8, 2, 64
NUM_BUCKETS, MAX_DISTANCE = 32, 2048
D_FF = 4 * D_MODEL
N_BRANCH = 3
B_SIDE = 64

LANES = 128
V_ROWS = 80
VMEM_LIMIT = 48 * 1024 * 1024
FLASH_ROW_GROUP = 128

LOG2_E = math.log2(math.e)

BF16 = jnp.bfloat16
F32 = jnp.float32


def _params(*sem, flags=None):
    return pltpu.CompilerParams(dimension_semantics=sem, vmem_limit_bytes=VMEM_LIMIT, flags=flags)


def _rms(x, g):
    return x * lax.rsqrt(jnp.mean(x * x, axis=-1, keepdims=True) + NORM_EPS) * g


_C_CQ = (0, 384)
_C_CKV = (384, 640)
_C_KR = (640, 768)
_C_KRR = (768, 896)
_C_QC = (896, 1920)
_C_QCR = (1920, 2944)
_C_KC = (2944, 3200)
_C_KCR = (3200, 3456)
_C_VC = (3456, 3712)
_W1_COLS = 3712


def _prep_ac_body(x_ref, gmix_ref, w1_ref, gq_ref, gkv_ref, wq2_ref, wkv2_ref, cg_ref, tab_ref,
                  qa_ref, ka_ref, va_ref, qc_ref, kc_ref, vc_ref):
    h = _rms(x_ref[...], gmix_ref[...]).astype(BF16)
    z = jnp.dot(h, w1_ref[...], preferred_element_type=F32)
    cqn = _rms(z[:, _C_CQ[0]:_C_CQ[1]], gq_ref[...]).astype(BF16)
    ckvn = _rms(z[:, _C_CKV[0]:_C_CKV[1]], gkv_ref[...]).astype(BF16)
    qq = jnp.dot(cqn, wq2_ref[...], preferred_element_type=F32)
    kv = jnp.dot(ckvn, wkv2_ref[...], preferred_element_type=F32)
    cos_q, sin_q, cos_k, sin_k, cos_c, sin_c = (tab_ref[t] for t in range(6))
    krope = z[:, _C_KR[0]:_C_KR[1]] * cos_k + z[:, _C_KRR[0]:_C_KRR[1]] * sin_k
    lane = lax.broadcasted_iota(jnp.int32, (1, LANES), 1)
    ones_col = (lane == A_V).astype(F32)
    hw = A_HEADS * LANES
    for hd in range(A_HEADS):
        sl = slice(hd * LANES, (hd + 1) * LANES)
        sr = slice(hw + hd * LANES, hw + (hd + 1) * LANES)
        qa_ref[:, sl] = (qq[:, sl] * cos_q + qq[:, sr] * sin_q).astype(BF16)
        ka_ref[:, sl] = (kv[:, sl] + krope).astype(BF16)
        va_ref[:, sl] = (kv[:, sr] + ones_col).astype(BF16)
    gq_cos = cg_ref[0:1, :] * cos_c
    gq_sin = cg_ref[1:2, :] * sin_c
    gk_cos = cg_ref[2:3, :] * cos_c
    gk_sin = cg_ref[3:4, :] * sin_c
    for hd in range(C_HEADS):
        sl = slice(hd * LANES, (hd + 1) * LANES)
        q = z[:, _C_QC[0] + hd * LANES:_C_QC[0] + (hd + 1) * LANES]
        qr = z[:, _C_QCR[0] + hd * LANES:_C_QCR[0] + (hd + 1) * LANES]
        inv = lax.rsqrt(jnp.sum(q * q, axis=-1, keepdims=True) * (1.0 / C_HD) + NORM_EPS)
        qc_ref[:, sl] = ((q * gq_cos + qr * gq_sin) * inv).astype(BF16)
    for hd in range(C_KV_HEADS):
        sl = slice(hd * LANES, (hd + 1) * LANES)
        k = z[:, _C_KC[0] + hd * LANES:_C_KC[0] + (hd + 1) * LANES]
        kr = z[:, _C_KCR[0] + hd * LANES:_C_KCR[0] + (hd + 1) * LANES]
        inv = lax.rsqrt(jnp.sum(k * k, axis=-1, keepdims=True) * (1.0 / C_HD) + NORM_EPS)
        kc_ref[:, sl] = ((k * gk_cos + kr * gk_sin) * inv).astype(BF16)
        vc_ref[:, sl] = (z[:, _C_VC[0] + hd * LANES:_C_VC[0] + (hd + 1) * LANES] + ones_col).astype(BF16)


def _prep_ac(x, gmix, w1, gq, gkv, wq2, wkv2, cg, tabs, *, seq, tm):
    rows = x.shape[0]
    tiles_per_seq = seq // tm
    full = lambda a: pl.BlockSpec(a.shape, lambda i: (0,) * a.ndim)
    row_spec = lambda w: pl.BlockSpec((tm, w), lambda i: (i, 0))
    wa, wc, wk = A_HEADS * LANES, C_HEADS * LANES, C_KV_HEADS * LANES
    return pl.pallas_call(
        _prep_ac_body,
        grid=(rows // tm,),
        in_specs=[row_spec(D_MODEL), full(gmix), full(w1), full(gq), full(gkv), full(wq2), full(wkv2), full(cg),
                  pl.BlockSpec((6, tm, LANES), lambda i: (0, i % tiles_per_seq, 0))],
        out_specs=[row_spec(wa), row_spec(wa), row_spec(wa), row_spec(wc), row_spec(wk), row_spec(wk)],
        out_shape=[jax.ShapeDtypeStruct((rows, w), BF16) for w in (wa, wa, wa, wc, wk, wk)],
        compiler_params=_params("arbitrary"),
        name="prep_ac",
    )(x, gmix, w1, gq, gkv, wq2, wkv2, cg, tabs)


def _rms_matmul_body(x_ref, g_ref, w_ref, o_ref):
    h = _rms(x_ref[...], g_ref[...]).astype(BF16)
    o_ref[...] = jnp.dot(h, w_ref[...], preferred_element_type=F32).astype(o_ref.dtype)


def _rms_matmul(x, g, w, out_dtype, *, tm, name):
    rows = x.shape[0]
    n = w.shape[1]
    return pl.pallas_call(
        _rms_matmul_body,
        grid=(rows // tm,),
        in_specs=[pl.BlockSpec((tm, D_MODEL), lambda i: (i, 0)),
                  pl.BlockSpec(g.shape, lambda i: (0, 0)),
                  pl.BlockSpec(w.shape, lambda i: (0, 0))],
        out_specs=pl.BlockSpec((tm, n), lambda i: (i, 0)),
        out_shape=jax.ShapeDtypeStruct((rows, n), out_dtype),
        compiler_params=_params("arbitrary"),
        name=name,
    )(x, g, w)


def _flash_body(qT_ref, k_ref, vT_ref, oT_ref, acc_ref, m_ref, s0_ref, s1_ref, mc0_ref, mc1_ref,
                p0_ref, p1_ref, al0_ref, al1_ref, *, tk, n_chunks, dv):
    m_ref[...] = jnp.full(m_ref.shape, NEG_BIG, F32)
    acc_ref[...] = jnp.zeros(acc_ref.shape, F32)

    tq = qT_ref.shape[1]
    rg = FLASH_ROW_GROUP
    n_groups = tk // rg
    kh = tk // 2

    def stage_a(c, g, s_ref, mc_ref):
        off = pl.multiple_of(c * tk + g * rg, rg)
        sg = jnp.dot(k_ref[pl.ds(off, rg), :], qT_ref[...], preferred_element_type=F32)
        s_ref[g * rg:(g + 1) * rg, :] = sg
        part = jnp.max(sg.reshape(rg // 8, 8, tq), axis=0)
        mc_ref[...] = part if g == 0 else jnp.maximum(mc_ref[...], part)

    def stage_b_head(mc_ref, al_ref):
        m_prev = m_ref[...]
        m_new = jnp.maximum(m_prev, jnp.max(mc_ref[...], axis=0, keepdims=True))
        al_ref[...] = jnp.exp2(m_prev - m_new)
        m_ref[...] = m_new

    def stage_b(g, s_ref, p_ref):
        rows = slice(g * rg, (g + 1) * rg)
        p_ref[rows, :] = jnp.exp2(s_ref[rows, :] - m_ref[...]).astype(BF16)

    def stage_c(c, half, p_ref, al_ref):
        rows = slice(half * kh, (half + 1) * kh)
        pv = jnp.dot(vT_ref[c, :, rows], p_ref[rows, :], preferred_element_type=F32)
        if half == 0:
            acc_ref[...] = acc_ref[...] * al_ref[...] + pv
        else:
            acc_ref[...] = acc_ref[...] + pv

    def step(c_a, c_c, s_in, mc_in, s_out, mc_out, p_in, al_in, p_out, al_out):
        stage_b_head(mc_in, al_out)
        for g in range(n_groups):
            if g % (n_groups // 2) == 0:
                stage_c(c_c, g // (n_groups // 2), p_in, al_in)
            stage_a(c_a, g, s_out, mc_out)
            stage_b(g, s_in, p_out)

    for g in range(n_groups):
        stage_a(0, g, s0_ref, mc0_ref)
    stage_b_head(mc0_ref, al0_ref)
    for g in range(n_groups):
        stage_a(1, g, s1_ref, mc1_ref)
        stage_b(g, s0_ref, p0_ref)

    def pair(j, carry):
        c0 = 2 * j
        step(jnp.minimum(c0 + 2, n_chunks - 1), c0, s1_ref, mc1_ref, s0_ref, mc0_ref, p0_ref, al0_ref, p1_ref, al1_ref)
        step(jnp.minimum(c0 + 3, n_chunks - 1), c0 + 1, s0_ref, mc0_ref, s1_ref, mc1_ref, p1_ref, al1_ref, p0_ref, al0_ref)
        return carry

    lax.fori_loop(0, n_chunks // 2, pair, 0)
    acc = acc_ref[...]
    oT_ref[...] = (acc[0:dv, :] * (1.0 / acc[dv:dv + 1, :])).astype(oT_ref.dtype)


def _flash(qT, k, vT, *, rep, tq, tk, name):
    nb, hq, _, seq = qT.shape
    n_chunks = seq // tk
    dv = A_V
    body = functools.partial(_flash_body, tk=tk, n_chunks=n_chunks, dv=dv)
    return pl.pallas_call(
        body,
        grid=(nb, hq, seq // tq),
        in_specs=[pl.BlockSpec((None, None, LANES, tq), lambda b, h, i: (b, h, 0, i)),
                  pl.BlockSpec((None, seq, LANES), lambda b, h, i: (b, 0, h // rep)),
                  pl.BlockSpec((None, None, n_chunks, V_ROWS, tk), lambda b, h, i: (b, h // rep, 0, 0, 0))],
        out_specs=pl.BlockSpec((None, None, dv, tq), lambda b, h, i: (b, h, 0, i)),
        out_shape=jax.ShapeDtypeStruct((nb, hq, dv, seq), BF16),
        scratch_shapes=[pltpu.VMEM((V_ROWS, tq), F32), pltpu.VMEM((1, tq), F32),
                        pltpu.VMEM((tk, tq), F32), pltpu.VMEM((tk, tq), F32),
                        pltpu.VMEM((8, tq), F32), pltpu.VMEM((8, tq), F32),
                        pltpu.VMEM((tk, tq), BF16), pltpu.VMEM((tk, tq), BF16),
                        pltpu.VMEM((1, tq), F32), pltpu.VMEM((1, tq), F32)],
        compiler_params=_params("arbitrary", "arbitrary", "arbitrary"),
        name=name,
    )(qT, k, vT)


B_TU = 128
B_WIN = B_TU + 2 * B_SIDE


def _dilated_body(q_ref, kp_ref, kc_ref, kn_ref, vp_ref, vc_ref, vn_ref, bias_ref, o_ref, lse_ref, *, n_u):
    u0 = pl.program_id(2) * B_TU
    kw = jnp.concatenate([kp_ref[B_TU - B_SIDE:, :], kc_ref[...], kn_ref[:B_SIDE, :]], axis=0)
    vw = jnp.concatenate([vp_ref[B_TU - B_SIDE:, :], vc_ref[...], vn_ref[:B_SIDE, :]], axis=0)
    key_u = u0 - B_SIDE + lax.broadcasted_iota(jnp.int32, (B_TU, B_WIN), 1)
    valid = (key_u >= 0) & (key_u < n_u)
    lane = lax.broadcasted_iota(jnp.int32, (B_TU, LANES), 1)
    low = lane < B_HD
    for hp in range(B_HEADS // 2):
        sl = slice(hp * LANES, (hp + 1) * LANES)
        qp = q_ref[:, sl]
        kp = kw[:, sl]
        vp = vw[:, sl]
        outs, lses = [], []
        for sub in range(2):
            qm = jnp.where(low if sub == 0 else jnp.logical_not(low), qp, jnp.zeros_like(qp))
            s = lax.dot_general(qm, kp, (((1,), (1,)), ((), ())), preferred_element_type=F32)
            s = jnp.where(valid, s + bias_ref[2 * hp + sub], NEG_BIG)
            m = jnp.max(s, axis=-1, keepdims=True)
            p = jnp.exp(s - m)
            l = jnp.sum(p, axis=-1, keepdims=True)
            o = jnp.dot(p.astype(BF16), vp, preferred_element_type=F32) * (1.0 / l)
            outs.append(o)
            lses.append(jnp.broadcast_to(m + jnp.log(l), (B_TU, LANES)))
        o_ref[:, sl] = jnp.where(low, outs[0], outs[1]).astype(o_ref.dtype)
        lse_ref[:, sl] = jnp.where(low, lses[0], lses[1])


def _dilated(zb, bias, *, group, dilation, seq, name):
    nb = zb.shape[0]
    n_u = seq // dilation
    nblk = n_u // B_TU
    hw = B_HEADS * B_HD
    ncol = 3 * B_GROUPS
    zv = zb.reshape(nb, n_u, dilation * ncol * hw)

    def spec(part, shift):
        def imap(b, r, i):
            blk = jnp.clip(i + shift, 0, nblk - 1)
            return (b, blk, r * ncol + 3 * group + part)
        return pl.BlockSpec((None, B_TU, hw), imap)

    out_spec = pl.BlockSpec((None, B_TU, hw), lambda b, r, i: (b, i, r))
    o, lse = pl.pallas_call(
        functools.partial(_dilated_body, n_u=n_u),
        grid=(nb, dilation, nblk),
        in_specs=[spec(0, 0), spec(1, -1), spec(1, 0), spec(1, 1), spec(2, -1), spec(2, 0), spec(2, 1),
                  pl.BlockSpec(bias.shape, lambda b, r, i: (0, 0, 0))],
        out_specs=[out_spec, out_spec],
        out_shape=[jax.ShapeDtypeStruct((nb, n_u, dilation * hw), BF16),
                   jax.ShapeDtypeStruct((nb, n_u, dilation * hw), F32)],
        compiler_params=_params("arbitrary", "arbitrary", "arbitrary"),
        name=name,
    )(zv, zv, zv, zv, zv, zv, zv, bias)
    return o.reshape(nb * seq, hw), lse.reshape(nb * seq, hw)


def _merge_body(x_ref, oa_ref, oc_ref, ob0_ref, ob1_ref, ob2_ref, l0_ref, l1_ref, l2_ref, gate_ref,
                wa_ref, wb_ref, wc_ref, wo_ref, out_ref):
    l0, l1, l2 = l0_ref[...], l1_ref[...], l2_ref[...]
    mx = jnp.maximum(jnp.maximum(l0, l1), l2)
    e0, e1, e2 = jnp.exp(l0 - mx), jnp.exp(l1 - mx), jnp.exp(l2 - mx)
    ob = (e0 * ob0_ref[...].astype(F32) + e1 * ob1_ref[...].astype(F32) + e2 * ob2_ref[...].astype(F32))
    ob = (ob * (1.0 / (e0 + e1 + e2))).astype(BF16)
    sg = 1.0 / (1.0 + jnp.exp(-gate_ref[...]))
    mix = sg[:, 0:D_MODEL] * jnp.dot(oa_ref[...], wa_ref[...], preferred_element_type=F32)
    mix += sg[:, D_MODEL:2 * D_MODEL] * jnp.dot(ob, wb_ref[...], preferred_element_type=F32)
    mix += sg[:, 2 * D_MODEL:] * jnp.dot(oc_ref[...], wc_ref[...], preferred_element_type=F32)
    out_ref[...] = x_ref[...] + jnp.dot(mix.astype(BF16), wo_ref[...], preferred_element_type=F32)


def _merge(x, oa, oc, obs, lses, gates, wa, wb, wc, wo, *, tm):
    rows = x.shape[0]
    row_spec = lambda w: pl.BlockSpec((tm, w), lambda i: (i, 0))
    full = lambda a: pl.BlockSpec(a.shape, lambda i: (0, 0))
    hw = B_HEADS * B_HD
    return pl.pallas_call(
        _merge_body,
        grid=(rows // tm,),
        in_specs=[row_spec(D_MODEL)] + [row_spec(hw)] * 8 + [row_spec(N_BRANCH * D_MODEL)]
                 + [full(wa), full(wb), full(wc), full(wo)],
        out_specs=row_spec(D_MODEL),
        out_shape=jax.ShapeDtypeStruct((rows, D_MODEL), F32),
        compiler_params=_params("arbitrary"),
        name="merge_out",
    )(x, oa, oc, *obs, *lses, gates, wa, wb, wc, wo)


FFN_CHUNK = 1024


def _ffn_body(x_ref, g_ref, wup_ref, wdn_ref, gfin_ref, out_ref, *, final):
    x = x_ref[...]
    h = _rms(x, g_ref[...]).astype(BF16)
    acc = x
    for c in range(D_FF // FFN_CHUNK):
        cs = slice(c * FFN_CHUNK, (c + 1) * FFN_CHUNK)
        u = jnp.dot(h, wup_ref[:, cs], preferred_element_type=F32)
        a = jnp.square(jnp.maximum(u, 0.0)).astype(BF16)
        acc = acc + jnp.dot(a, wdn_ref[cs, :], preferred_element_type=F32)
    if final:
        acc = _rms(acc, gfin_ref[...])
    out_ref[...] = acc


def _ffn(x, g, wup, wdn, gfin, *, final, tm):
    rows = x.shape[0]
    full = lambda a: pl.BlockSpec(a.shape, lambda i: (0, 0))
    return pl.pallas_call(
        functools.partial(_ffn_body, final=final),
        grid=(rows // tm,),
        in_specs=[pl.BlockSpec((tm, D_MODEL), lambda i: (i, 0)), full(g), full(wup), full(wdn), full(gfin)],
        out_specs=pl.BlockSpec((tm, D_MODEL), lambda i: (i, 0)),
        out_shape=jax.ShapeDtypeStruct((rows, D_MODEL), F32),
        compiler_params=_params("arbitrary"),
        name="ffn_final" if final else "ffn",
    )(x, g, wup, wdn, gfin)


def _pad_heads(w, n_heads, hd, lane_off=0):
    k = w.shape[0]
    w = w.reshape(k, n_heads, hd)
    w = jnp.pad(w, ((0, 0), (0, 0), (lane_off, LANES - lane_off - hd)))
    return w.reshape(k, n_heads * LANES)


def _rot_cols(w):
    half = w.shape[-1] // 2
    return jnp.concatenate([-w[..., half:], w[..., :half]], axis=-1)


def _rot_axial(w):
    half = C_HD // 2
    return jnp.concatenate([_rot_cols(w[..., :half]), _rot_cols(w[..., half:])], axis=-1)


def _swap_axial(g):
    q = C_HD // 4
    return jnp.concatenate([g[q:2 * q], g[:q], g[3 * q:], g[2 * q:3 * q]], axis=-1)


def _rope_angles(pos, half):
    freqs = ROPE_THETA ** (-jnp.arange(half, dtype=F32) / half)
    return pos.astype(F32)[:, None] * freqs[None, :]


def _tables(seq):
    pos = jnp.arange(seq, dtype=jnp.int32)
    ang = _rope_angles(pos, A_ROPE // 2)
    cos_r = jnp.tile(jnp.cos(ang), (1, 2))
    sin_r = jnp.tile(jnp.sin(ang), (1, 2))
    zeros = lambda w: jnp.zeros((seq, w), F32)
    cos_k = jnp.concatenate([zeros(A_NOPE), cos_r, zeros(LANES - A_NOPE - A_ROPE)], axis=1)
    sin_k = jnp.concatenate([zeros(A_NOPE), sin_r, zeros(LANES - A_NOPE - A_ROPE)], axis=1)
    scale_a = (A_NOPE + A_ROPE) ** -0.5 * LOG2_E
    nope = jnp.concatenate([jnp.ones((seq, A_NOPE), F32), zeros(LANES - A_NOPE)], axis=1)
    cos_q = (nope + cos_k) * scale_a
    sin_q = sin_k * scale_a
    quarter = C_HD // 4
    ang_r = _rope_angles(pos // GRID_W, quarter)
    ang_c = _rope_angles(pos % GRID_W, quarter)
    cos_c = jnp.concatenate([jnp.tile(jnp.cos(ang_r), (1, 2)), jnp.tile(jnp.cos(ang_c), (1, 2)), zeros(LANES - C_HD)], axis=1)
    sin_c = jnp.concatenate([jnp.tile(jnp.sin(ang_r), (1, 2)), jnp.tile(jnp.sin(ang_c), (1, 2)), zeros(LANES - C_HD)], axis=1)
    return jnp.stack([cos_q, sin_q, cos_k, sin_k, cos_c, sin_c], axis=0)


def _t5_bucket(rel):
    half = NUM_BUCKETS // 2
    max_exact = half // 2
    ret = jnp.where(rel > 0, half, 0)
    n = jnp.abs(rel)
    nf = jnp.maximum(n, 1).astype(F32)
    large = max_exact + (jnp.log(nf / max_exact) / math.log(MAX_DISTANCE / max_exact) * (half - max_exact)).astype(jnp.int32)
    large = jnp.minimum(large, half - 1)
    return ret + jnp.where(n < max_exact, n, large)


def _band_bias(t5_table, group, dilation):
    offs = dilation * jnp.arange(-B_SIDE, B_SIDE + 1, dtype=jnp.int32)
    bias_hj = t5_table[_t5_bucket(offs)][:, group * B_HEADS:(group + 1) * B_HEADS].T.astype(F32)
    rel = jnp.arange(B_WIN, dtype=jnp.int32)[None, :] - jnp.arange(B_TU, dtype=jnp.int32)[:, None]
    inside = (rel >= 0) & (rel <= 2 * B_SIDE)
    vals = bias_hj[:, jnp.clip(rel, 0, 2 * B_SIDE)]
    return jnp.where(inside[None], vals, NEG_BIG)


def _layer_weights(l, norm_mix, w_in, a_q_norm, a_kv_norm, a_w_uq, a_w_ukv, c_q_norm, c_k_norm,
                   w_br_a, w_br_b, w_br_c, w_out, norm_ffn, w_up, w_down):
    w = w_in[l]
    o = 0
    cols = []
    for width in (A_Q_LORA, A_KV_LORA, A_ROPE, 3 * B_GROUPS * B_HEADS * B_HD, C_HEADS * C_HD,
                  C_KV_HEADS * C_HD, C_KV_HEADS * C_HD, N_BRANCH * D_MODEL):
        cols.append(w[:, o:o + width])
        o += width
    w_cq, w_ckv, w_kr, w_b, w_qc, w_kc, w_vc, w_gate = cols
    kr_p = _pad_heads(w_kr, 1, A_ROPE, A_NOPE)
    kr_rot_p = _pad_heads(_rot_cols(w_kr), 1, A_ROPE, A_NOPE)
    qc = w_qc.reshape(D_MODEL, C_HEADS, C_HD)
    kc = w_kc.reshape(D_MODEL, C_KV_HEADS, C_HD)
    w1 = jnp.concatenate([
        w_cq, w_ckv, kr_p, kr_rot_p,
        _pad_heads(w_qc, C_HEADS, C_HD), _pad_heads(_rot_axial(qc).reshape(D_MODEL, -1), C_HEADS, C_HD),
        _pad_heads(w_kc, C_KV_HEADS, C_HD), _pad_heads(_rot_axial(kc).reshape(D_MODEL, -1), C_KV_HEADS, C_HD),
        _pad_heads(w_vc, C_KV_HEADS, C_HD)], axis=1).astype(BF16)
    uq = a_w_uq[l].reshape(A_Q_LORA, A_HEADS, A_NOPE + A_ROPE)
    uq_rot = jnp.concatenate([jnp.zeros((A_Q_LORA, A_HEADS, A_NOPE), F32), _rot_cols(uq[..., A_NOPE:])], axis=-1)
    wq2 = jnp.concatenate([_pad_heads(uq.reshape(A_Q_LORA, -1), A_HEADS, A_NOPE + A_ROPE),
                           _pad_heads(uq_rot.reshape(A_Q_LORA, -1), A_HEADS, A_NOPE + A_ROPE)], axis=1).astype(BF16)
    ukv = a_w_ukv[l].reshape(A_KV_LORA, A_HEADS, A_NOPE + A_V)
    wkv2 = jnp.concatenate([_pad_heads(ukv[..., :A_NOPE].reshape(A_KV_LORA, -1), A_HEADS, A_NOPE),
                            _pad_heads(ukv[..., A_NOPE:].reshape(A_KV_LORA, -1), A_HEADS, A_V)], axis=1).astype(BF16)
    pad_gain = lambda g: jnp.pad(g, (0, LANES - C_HD))
    scale_c = C_HD ** -0.5 * LOG2_E
    cg = jnp.stack([pad_gain(c_q_norm[l]) * scale_c, pad_gain(_swap_axial(c_q_norm[l])) * scale_c,
                    pad_gain(c_k_norm[l]), pad_gain(_swap_axial(c_k_norm[l]))], axis=0)
    wb5 = w_b.reshape(D_MODEL, 3, B_GROUPS, B_HEADS * B_HD)
    wb5 = wb5 * jnp.array([B_HD ** -0.5, 1.0, 1.0], F32)[None, :, None, None]
    w_bq = jnp.transpose(wb5, (0, 2, 1, 3)).reshape(D_MODEL, -1).astype(BF16)
    return dict(
        gmix=norm_mix[l][None, :], w1=w1, gq=a_q_norm[l][None, :], gkv=a_kv_norm[l][None, :], wq2=wq2, wkv2=wkv2,
        cg=cg, w_bq=w_bq, w_gate=w_gate.astype(BF16),
        wa=w_br_a[l].astype(BF16), wb=w_br_b[l].astype(BF16), wc=w_br_c[l].astype(BF16), wo=w_out[l].astype(BF16),
        gffn=norm_ffn[l][None, :], wup=w_up[l].astype(BF16), wdn=w_down[l].astype(BF16))


def _heads_to_lanes(a, nb, seq, n_heads):
    return jnp.transpose(a.reshape(nb, seq, n_heads, LANES), (0, 2, 3, 1))


def _value_chunks(v, nb, seq, n_heads, tk):
    vt = _heads_to_lanes(v, nb, seq, n_heads)[:, :, :V_ROWS, :]
    return jnp.transpose(vt.reshape(nb, n_heads, V_ROWS, seq // tk, tk), (0, 1, 3, 2, 4))


def _encoder_layer(x, wts, tabs, biases, *, nb, seq, final, final_norm, tm, tq, tk):
    qa, ka, va, qc, kc, vc = _prep_ac(x, wts["gmix"], wts["w1"], wts["gq"], wts["gkv"], wts["wq2"], wts["wkv2"],
                                      wts["cg"], tabs, seq=seq, tm=tm)
    zb = _rms_matmul(x, wts["gmix"], wts["w_bq"], BF16, tm=tm, name="proj_b")
    gates = _rms_matmul(x, wts["gmix"], wts["w_gate"], F32, tm=tm, name="proj_gate")

    oa_t = _flash(_heads_to_lanes(qa, nb, seq, A_HEADS), ka.reshape(nb, seq, -1),
                  _value_chunks(va, nb, seq, A_HEADS, tk), rep=1, tq=tq, tk=tk, name="flash_a")
    oc_t = _flash(_heads_to_lanes(qc, nb, seq, C_HEADS), kc.reshape(nb, seq, -1),
                  _value_chunks(vc, nb, seq, C_KV_HEADS, tk), rep=C_HEADS // C_KV_HEADS, tq=tq, tk=tk, name="flash_c")
    to_rows = lambda o_t: jnp.transpose(o_t, (0, 3, 1, 2)).reshape(nb * seq, -1)
    oa, oc = to_rows(oa_t), to_rows(oc_t)

    zb3 = zb.reshape(nb, seq, -1)
    obs, lses = [], []
    for g, (_, dilation) in enumerate(B_PAIRS):
        o_g, lse_g = _dilated(zb3, biases[g], group=g, dilation=dilation, seq=seq, name=f"dilated_{g}")
        obs.append(o_g)
        lses.append(lse_g)

    x = _merge(x, oa, oc, obs, lses, gates, wts["wa"], wts["wb"], wts["wc"], wts["wo"], tm=tm)
    return _ffn(x, wts["gffn"], wts["wup"], wts["wdn"], final_norm, final=final, tm=tm)


def _trunk(x, norm_mix, w_in, a_q_norm, a_kv_norm, a_w_uq, a_w_ukv, c_q_norm, c_k_norm,
           w_br_a, w_br_b, w_br_c, w_out, norm_ffn, w_up, w_down, t5_table, final_norm, *, tm, tq, tk):
    nb, seq, _ = x.shape
    tabs = _tables(seq)
    biases = [_band_bias(t5_table, g, d) for g, (_, d) in enumerate(B_PAIRS)]
    depth = w_in.shape[0]
    xr = x.reshape(nb * seq, D_MODEL)
    for l in range(depth):
        wts = _layer_weights(l, norm_mix, w_in, a_q_norm, a_kv_norm, a_w_uq, a_w_ukv, c_q_norm, c_k_norm,
                             w_br_a, w_br_b, w_br_c, w_out, norm_ffn, w_up, w_down)
        xr = _encoder_layer(xr, wts, tabs, biases, nb=nb, seq=seq, final=(l == depth - 1),
                            final_norm=final_norm[None, :], tm=tm, tq=tq, tk=tk)
    return xr.reshape(nb, seq, D_MODEL)


def kernel(x_prompt, x_sample, norm_mix, w_in, a_q_norm, a_kv_norm, a_w_uq, a_w_ukv, c_q_norm, c_k_norm,
           w_br_a, w_br_b, w_br_c, w_out, norm_ffn, w_up, w_down, t5_table, final_norm):
    assert x_prompt.shape[1:] == x_sample.shape[1:]
    n_prompt = x_prompt.shape[0]
    x = jnp.concatenate([x_prompt, x_sample], axis=0)
    y = _trunk(x, norm_mix, w_in, a_q_norm, a_kv_norm, a_w_uq, a_w_ukv, c_q_norm, c_k_norm,
               w_br_a, w_br_b, w_br_c, w_out, norm_ffn, w_up, w_down, t5_table, final_norm,
               tm=256, tq=1024, tk=512)
    return (y[:n_prompt], y[n_prompt:])
```

```python
import functools
import math

import jax
import jax.numpy as jnp
import numpy as np
from jax import lax
from jax.experimental import pallas as pl
from jax.experimental.pallas import tpu as pltpu

D_MODEL = 1024
DEPTH = 2
GRID_W = 64
NORM_EPS = 1e-6
ROPE_THETA = 10000.0
NEG_BIG = -1e30
A_HEADS, A_NOPE, A_ROPE, A_V = 8, 64, 32, 64
A_Q_LORA, A_KV_LORA = 384, 256
B_PAIRS = ((128, 1), (512, 4), (2048, 16))
B_GROUPS, B_HEADS, B_HD = 3, 8, 64
C_HEADS, C_KV_HEADS, C_HD = 8, 2, 64
NUM_BUCKETS, MAX_DISTANCE = 32, 2048
D_FF = 4 * D_MODEL
N_BRANCH = 3
B_SIDE = 64

LANES = 128
VMEM_LIMIT = 48 * 1024 * 1024
FLASH_ROW_GROUP = 128

LOG2_E = math.log2(math.e)

BF16 = jnp.bfloat16
F32 = jnp.float32


def _params(*sem, flags=None):
    return pltpu.CompilerParams(dimension_semantics=sem, vmem_limit_bytes=VMEM_LIMIT, flags=flags)


def _rms(x, g):
    return x * lax.rsqrt(jnp.mean(x * x, axis=-1, keepdims=True) + NORM_EPS) * g


_C_CQ = (0, 384)
_C_CKV = (384, 640)
_C_KR = (640, 768)
_C_KRR = (768, 896)
_C_QC = (896, 1920)
_C_QCR = (1920, 2944)
_C_KC = (2944, 3200)
_C_KCR = (3200, 3456)
_C_VC = (3456, 3712)
_W1_COLS = 3712


def _prep_ac_body(x_ref, gmix_ref, w1_ref, gq_ref, gkv_ref, wq2_ref, wkv2_ref, cg_ref, tab_ref,
                  qa_ref, ka_ref, va_ref, qc_ref, kc_ref, vc_ref):
    h = _rms(x_ref[...], gmix_ref[...]).astype(BF16)
    z = jnp.dot(h, w1_ref[...], preferred_element_type=F32)
    cqn = _rms(z[:, _C_CQ[0]:_C_CQ[1]], gq_ref[...]).astype(BF16)
    ckvn = _rms(z[:, _C_CKV[0]:_C_CKV[1]], gkv_ref[...]).astype(BF16)
    qq = jnp.dot(cqn, wq2_ref[...], preferred_element_type=F32)
    kv = jnp.dot(ckvn, wkv2_ref[...], preferred_element_type=F32)
    cos_q, sin_q, cos_k, sin_k, cos_c, sin_c = (tab_ref[t] for t in range(6))
    krope = z[:, _C_KR[0]:_C_KR[1]] * cos_k + z[:, _C_KRR[0]:_C_KRR[1]] * sin_k
    lane = lax.broadcasted_iota(jnp.int32, (1, LANES), 1)
    ones_col = (lane == A_V).astype(F32)
    hw = A_HEADS * LANES
    for hd in range(A_HEADS):
        sl = slice(hd * LANES, (hd + 1) * LANES)
        sr = slice(hw + hd * LANES, hw + (hd + 1) * LANES)
        qa_ref[:, sl] = (qq[:, sl] * cos_q + qq[:, sr] * sin_q).astype(BF16)
        ka_ref[:, sl] = (kv[:, sl] + krope).astype(BF16)
        va_ref[:, sl] = (kv[:, sr] + ones_col).astype(BF16)
    gq_cos = cg_ref[0:1, :] * cos_c
    gq_sin = cg_ref[1:2, :] * sin_c
    gk_cos = cg_ref[2:3, :] * cos_c
    gk_sin = cg_ref[3:4, :] * sin_c
    for hd in range(C_HEADS):
        sl = slice(hd * LANES, (hd + 1) * LANES)
        q = z[:, _C_QC[0] + hd * LANES:_C_QC[0] + (hd + 1) * LANES]
        qr = z[:, _C_QCR[0] + hd * LANES:_C_QCR[0] + (hd + 1) * LANES]
        inv = lax.rsqrt(jnp.sum(q * q, axis=-1, keepdims=True) * (1.0 / C_HD) + NORM_EPS)
        qc_ref[:, sl] = ((q * gq_cos + qr * gq_sin) * inv).astype(BF16)
    for hd in range(C_KV_HEADS):
        sl = slice(hd * LANES, (hd + 1) * LANES)
        k = z[:, _C_KC[0] + hd * LANES:_C_KC[0] + (hd + 1) * LANES]
        kr = z[:, _C_KCR[0] + hd * LANES:_C_KCR[0] + (hd + 1) * LANES]
        inv = lax.rsqrt(jnp.sum(k * k, axis=-1, keepdims=True) * (1.0 / C_HD) + NORM_EPS)
        kc_ref[:, sl] = ((k * gk_cos + kr * gk_sin) * inv).astype(BF16)
        vc_ref[:, sl] = (z[:, _C_VC[0] + hd * LANES:_C_VC[0] + (hd + 1) * LANES] + ones_col).astype(BF16)


def _prep_ac(x, gmix, w1, gq, gkv, wq2, wkv2, cg, tabs, *, seq, tm):
    rows = x.shape[0]
    tiles_per_seq = seq // tm
    full = lambda a: pl.BlockSpec(a.shape, lambda i: (0,) * a.ndim)
    row_spec = lambda w: pl.BlockSpec((tm, w), lambda i: (i, 0))
    wa, wc, wk = A_HEADS * LANES, C_HEADS * LANES, C_KV_HEADS * LANES
    return pl.pallas_call(
        _prep_ac_body,
        grid=(rows // tm,),
        in_specs=[row_spec(D_MODEL), full(gmix), full(w1), full(gq), full(gkv), full(wq2), full(wkv2), full(cg),
                  pl.BlockSpec((6, tm, LANES), lambda i: (0, i % tiles_per_seq, 0))],
        out_specs=[row_spec(wa), row_spec(wa), row_spec(wa), row_spec(wc), row_spec(wk), row_spec(wk)],
        out_shape=[jax.ShapeDtypeStruct((rows, w), BF16) for w in (wa, wa, wa, wc, wk, wk)],
        compiler_params=_params("arbitrary"),
        name="prep_ac",
    )(x, gmix, w1, gq, gkv, wq2, wkv2, cg, tabs)


def _rms_matmul_body(x_ref, g_ref, w_ref, o_ref):
    h = _rms(x_ref[...], g_ref[...]).astype(BF16)
    o_ref[...] = jnp.dot(h, w_ref[...], preferred_element_type=F32).astype(o_ref.dtype)


def _rms_matmul(x, g, w, out_dtype, *, tm, name):
    rows = x.shape[0]
    n = w.shape[1]
    return pl.pallas_call(
        _rms_matmul_body,
        grid=(rows // tm,),
        in_specs=[pl.BlockSpec((tm, D_MODEL), lambda i: (i, 0)),
                  pl.BlockSpec(g.shape, lambda i: (0, 0)),
                  pl.BlockSpec(w.shape, lambda i: (0, 0))],
        out_specs=pl.BlockSpec((tm, n), lambda i: (i, 0)),
        out_shape=jax.ShapeDtypeStruct((rows, n), out_dtype),
        compiler_params=_params("arbitrary"),
        name=name,
    )(x, g, w)


def _flash_body(q_ref, k_ref, v_ref, o_ref, qT_ref, vT_ref, acc_ref, m_ref, s0_ref, s1_ref, mc0_ref, mc1_ref,
                p0_ref, p1_ref, al0_ref, al1_ref, *, tk, n_chunks):
    tq = q_ref.shape[0]
    rg = FLASH_ROW_GROUP
    n_groups = tk // rg
    kh = vT_ref.shape[2]
    assert tk == 2 * kh

    @pl.when(pl.program_id(2) == 0)
    def _():
        def tr(n, carry):
            off = pl.multiple_of(n * kh, kh)
            vT_ref[n] = v_ref[pl.ds(off, kh), :].astype(F32).T.astype(BF16)
            return carry
        lax.fori_loop(0, vT_ref.shape[0], tr, 0)

    qT_ref[...] = q_ref[...].astype(F32).T.astype(BF16)
    m_ref[...] = jnp.full(m_ref.shape, NEG_BIG, F32)
    acc_ref[...] = jnp.zeros(acc_ref.shape, F32)

    def stage_a(c, g, s_ref, mc_ref):
        off = pl.multiple_of(c * tk + g * rg, rg)
        sg = jnp.dot(k_ref[pl.ds(off, rg), :], qT_ref[...], preferred_element_type=F32)
        s_ref[g * rg:(g + 1) * rg, :] = sg
        part = jnp.max(sg.reshape(rg // 8, 8, tq), axis=0)
        mc_ref[...] = part if g == 0 else jnp.maximum(mc_ref[...], part)

    def stage_b_head(mc_ref, al_ref):
        m_prev = m_ref[...]
        m_new = jnp.maximum(m_prev, jnp.max(mc_ref[...], axis=0, keepdims=True))
        al_ref[...] = jnp.exp2(m_prev - m_new)
        m_ref[...] = m_new

    def stage_b(g, s_ref, p_ref):
        rows = slice(g * rg, (g + 1) * rg)
        p_ref[rows, :] = jnp.exp2(s_ref[rows, :] - m_ref[...]).astype(BF16)

    def stage_c(c, half, p_ref, al_ref):
        rows = slice(half * kh, (half + 1) * kh)
        pv = jnp.dot(vT_ref[2 * c + half], p_ref[rows, :], preferred_element_type=F32)
        if half == 0:
            acc_ref[...] = acc_ref[...] * al_ref[...] + pv
        else:
            acc_ref[...] = acc_ref[...] + pv

    def step(c_a, c_c, s_in, mc_in, s_out, mc_out, p_in, al_in, p_out, al_out):
        stage_b_head(mc_in, al_out)
        for g in range(n_groups):
            if g % (n_groups // 2) == 0:
                stage_c(c_c, g // (n_groups // 2), p_in, al_in)
            stage_a(c_a, g, s_out, mc_out)
            stage_b(g, s_in, p_out)

    for g in range(n_groups):
        stage_a(0, g, s0_ref, mc0_ref)
    stage_b_head(mc0_ref, al0_ref)
    for g in range(n_groups):
        stage_a(1, g, s1_ref, mc1_ref)
        stage_b(g, s0_ref, p0_ref)

    def pair(j, carry):
        c0 = 2 * j
        step(jnp.minimum(c0 + 2, n_chunks - 1), c0, s1_ref, mc1_ref, s0_ref, mc0_ref, p0_ref, al0_ref, p1_ref, al1_ref)
        step(jnp.minimum(c0 + 3, n_chunks - 1), c0 + 1, s0_ref, mc0_ref, s1_ref, mc1_ref, p1_ref, al1_ref, p0_ref, al0_ref)
        return carry

    lax.fori_loop(0, n_chunks // 2, pair, 0)
    acc = acc_ref[...]
    o_ref[...] = (acc * (1.0 / acc[A_V:A_V + 1, :])).T.astype(o_ref.dtype)


def _flash(q, k, v, *, rep, tq, tk, name):
    nb, seq, wq = q.shape
    hq = wq // LANES
    n_chunks = seq // tk
    kh = tk // 2
    assert n_chunks % 2 == 0
    body = functools.partial(_flash_body, tk=tk, n_chunks=n_chunks)
    return pl.pallas_call(
        body,
        grid=(nb, hq, seq // tq),
        in_specs=[pl.BlockSpec((None, tq, LANES), lambda b, h, i: (b, i, h)),
                  pl.BlockSpec((None, seq, LANES), lambda b, h, i: (b, 0, h // rep)),
                  pl.BlockSpec((None, seq, LANES), lambda b, h, i: (b, 0, h // rep))],
        out_specs=pl.BlockSpec((None, tq, LANES), lambda b, h, i: (b, i, h)),
        out_shape=jax.ShapeDtypeStruct((nb, seq, wq), BF16),
        scratch_shapes=[pltpu.VMEM((LANES, tq), BF16), pltpu.VMEM((seq // kh, LANES, kh), BF16),
                        pltpu.VMEM((LANES, tq), F32), pltpu.VMEM((1, tq), F32),
                        pltpu.VMEM((tk, tq), F32), pltpu.VMEM((tk, tq), F32),
                        pltpu.VMEM((8, tq), F32), pltpu.VMEM((8, tq), F32),
                        pltpu.VMEM((tk, tq), BF16), pltpu.VMEM((tk, tq), BF16),
                        pltpu.VMEM((1, tq), F32), pltpu.VMEM((1, tq), F32)],
        compiler_params=_params("arbitrary", "arbitrary", "arbitrary"),
        name=name,
    )(q, k, v)


B_TU = 128
B_WIN = B_TU + 2 * B_SIDE


def _dilated_body(q_ref, kp_ref, kc_ref, kn_ref, vp_ref, vc_ref, vn_ref, bias_ref, o_ref, lse_ref, *, n_u):
    u0 = pl.program_id(2) * B_TU
    kw = jnp.concatenate([kp_ref[B_TU - B_SIDE:, :], kc_ref[...], kn_ref[:B_SIDE, :]], axis=0)
    vw = jnp.concatenate([vp_ref[B_TU - B_SIDE:, :], vc_ref[...], vn_ref[:B_SIDE, :]], axis=0)
    key_u = u0 - B_SIDE + lax.broadcasted_iota(jnp.int32, (B_TU, B_WIN), 1)
    valid = (key_u >= 0) & (key_u < n_u)
    lane = lax.broadcasted_iota(jnp.int32, (B_TU, LANES), 1)
    low = lane < B_HD
    for hp in range(B_HEADS // 2):
        sl = slice(hp * LANES, (hp + 1) * LANES)
        qp = q_ref[:, sl]
        kp = kw[:, sl]
        vp = vw[:, sl]
        outs, lses = [], []
        for sub in range(2):
            qm = jnp.where(low if sub == 0 else jnp.logical_not(low), qp, jnp.zeros_like(qp))
            s = lax.dot_general(qm, kp, (((1,), (1,)), ((), ())), preferred_element_type=F32)
            s = jnp.where(valid, s + bias_ref[2 * hp + sub], NEG_BIG)
            m = jnp.max(s, axis=-1, keepdims=True)
            p = jnp.exp(s - m)
            l = jnp.sum(p, axis=-1, keepdims=True)
            o = jnp.dot(p.astype(BF16), vp, preferred_element_type=F32) * (1.0 / l)
            outs.append(o)
            lses.append(jnp.broadcast_to(m + jnp.log(l), (B_TU, LANES)))
        o_ref[:, sl] = jnp.where(low, outs[0], outs[1]).astype(o_ref.dtype)
        lse_ref[:, sl] = jnp.where(low, lses[0], lses[1])


def _dilated(zb, bias, *, group, dilation, seq, name):
    nb = zb.shape[0]
    n_u = seq // dilation
    nblk = n_u // B_TU
    hw = B_HEADS * B_HD
    ncol = 3 * B_GROUPS
    zv = zb.reshape(nb, n_u, dilation * ncol * hw)

    def spec(part, shift):
        def imap(b, r, i):
            blk = jnp.clip(i + shift, 0, nblk - 1)
            return (b, blk, r * ncol + 3 * group + part)
        return pl.BlockSpec((None, B_TU, hw), imap)

    out_spec = pl.BlockSpec((None, B_TU, hw), lambda b, r, i: (b, i, r))
    o, lse = pl.pallas_call(
        functools.partial(_dilated_body, n_u=n_u),
        grid=(nb, dilation, nblk),
        in_specs=[spec(0, 0), spec(1, -1), spec(1, 0), spec(1, 1), spec(2, -1), spec(2, 0), spec(2, 1),
                  pl.BlockSpec(bias.shape, lambda b, r, i: (0, 0, 0))],
        out_specs=[out_spec, out_spec],
        out_shape=[jax.ShapeDtypeStruct((nb, n_u, dilation * hw), BF16),
                   jax.ShapeDtypeStruct((nb, n_u, dilation * hw), F32)],
        compiler_params=_params("arbitrary", "arbitrary", "arbitrary"),
        name=name,
    )(zv, zv, zv, zv, zv, zv, zv, bias)
    return o.reshape(nb * seq, hw), lse.reshape(nb * seq, hw)


def _merge_body(x_ref, oa_ref, oc_ref, ob0_ref, ob1_ref, ob2_ref, l0_ref, l1_ref, l2_ref, gate_ref,
                wa_ref, wb_ref, wc_ref, wo_ref, out_ref):
    l0, l1, l2 = l0_ref[...], l1_ref[...], l2_ref[...]
    mx = jnp.maximum(jnp.maximum(l0, l1), l2)
    e0, e1, e2 = jnp.exp(l0 - mx), jnp.exp(l1 - mx), jnp.exp(l2 - mx)
    ob = (e0 * ob0_ref[...].astype(F32) + e1 * ob1_ref[...].astype(F32) + e2 * ob2_ref[...].astype(F32))
    ob = (ob * (1.0 / (e0 + e1 + e2))).astype(BF16)
    sg = 1.0 / (1.0 + jnp.exp(-gate_ref[...]))
    mix = sg[:, 0:D_MODEL] * jnp.dot(oa_ref[...], wa_ref[...], preferred_element_type=F32)
    mix += sg[:, D_MODEL:2 * D_MODEL] * jnp.dot(ob, wb_ref[...], preferred_element_type=F32)
    mix += sg[:, 2 * D_MODEL:] * jnp.dot(oc_ref[...], wc_ref[...], preferred_element_type=F32)
    out_ref[...] = x_ref[...] + jnp.dot(mix.astype(BF16), wo_ref[...], preferred_element_type=F32)


def _merge(x, oa, oc, obs, lses, gates, wa, wb, wc, wo, *, tm):
    rows = x.shape[0]
    row_spec = lambda w: pl.BlockSpec((tm, w), lambda i: (i, 0))
    full = lambda a: pl.BlockSpec(a.shape, lambda i: (0, 0))
    hw = B_HEADS * B_HD
    return pl.pallas_call(
        _merge_body,
        grid=(rows // tm,),
        in_specs=[row_spec(D_MODEL), row_spec(oa.shape[1]), row_spec(oc.shape[1])] + [row_spec(hw)] * 6
                 + [row_spec(N_BRANCH * D_MODEL)]
                 + [full(wa), full(wb), full(wc), full(wo)],
        out_specs=row_spec(D_MODEL),
        out_shape=jax.ShapeDtypeStruct((rows, D_MODEL), F32),
        compiler_params=_params("arbitrary"),
        name="merge_out",
    )(x, oa, oc, *obs, *lses, gates, wa, wb, wc, wo)


FFN_CHUNK = 1024


def _ffn_body(x_ref, g_ref, wup_ref, wdn_ref, gfin_ref, out_ref, *, final):
    x = x_ref[...]
    h = _rms(x, g_ref[...]).astype(BF16)
    acc = x
    for c in range(D_FF // FFN_CHUNK):
        cs = slice(c * FFN_CHUNK, (c + 1) * FFN_CHUNK)
        u = jnp.dot(h, wup_ref[:, cs], preferred_element_type=F32)
        a = jnp.square(jnp.maximum(u, 0.0)).astype(BF16)
        acc = acc + jnp.dot(a, wdn_ref[cs, :], preferred_element_type=F32)
    if final:
        acc = _rms(acc, gfin_ref[...])
    out_ref[...] = acc


def _ffn(x, g, wup, wdn, gfin, *, final, tm):
    rows = x.shape[0]
    full = lambda a: pl.BlockSpec(a.shape, lambda i: (0, 0))
    return pl.pallas_call(
        functools.partial(_ffn_body, final=final),
        grid=(rows // tm,),
        in_specs=[pl.BlockSpec((tm, D_MODEL), lambda i: (i, 0)), full(g), full(wup), full(wdn), full(gfin)],
        out_specs=pl.BlockSpec((tm, D_MODEL), lambda i: (i, 0)),
        out_shape=jax.ShapeDtypeStruct((rows, D_MODEL), F32),
        compiler_params=_params("arbitrary"),
        name="ffn_final" if final else "ffn",
    )(x, g, wup, wdn, gfin)


def _pad_heads(w, n_heads, hd, lane_off=0):
    k = w.shape[0]
    w = w.reshape(k, n_heads, hd)
    w = jnp.pad(w, ((0, 0), (0, 0), (lane_off, LANES - lane_off - hd)))
    return w.reshape(k, n_heads * LANES)


def _pad_head_rows(w, n_heads, hd):
    n = w.shape[1]
    return jnp.pad(w.reshape(n_heads, hd, n), ((0, 0), (0, LANES - hd), (0, 0))).reshape(n_heads * LANES, n)


def _rot_cols(w):
    half = w.shape[-1] // 2
    return jnp.concatenate([-w[..., half:], w[..., :half]], axis=-1)


def _rot_axial(w):
    half = C_HD // 2
    return jnp.concatenate([_rot_cols(w[..., :half]), _rot_cols(w[..., half:])], axis=-1)


def _swap_axial(g):
    q = C_HD // 4
    return jnp.concatenate([g[q:2 * q], g[:q], g[3 * q:], g[2 * q:3 * q]], axis=-1)


def _rope_angles(pos, half):
    freqs = ROPE_THETA ** (-jnp.arange(half, dtype=F32) / half)
    return pos.astype(F32)[:, None] * freqs[None, :]


def _tables(seq):
    pos = jnp.arange(seq, dtype=jnp.int32)
    ang = _rope_angles(pos, A_ROPE // 2)
    cos_r = jnp.tile(jnp.cos(ang), (1, 2))
    sin_r = jnp.tile(jnp.sin(ang), (1, 2))
    zeros = lambda w: jnp.zeros((seq, w), F32)
    cos_k = jnp.concatenate([zeros(A_NOPE), cos_r, zeros(LANES - A_NOPE - A_ROPE)], axis=1)
    sin_k = jnp.concatenate([zeros(A_NOPE), sin_r, zeros(LANES - A_NOPE - A_ROPE)], axis=1)
    scale_a = (A_NOPE + A_ROPE) ** -0.5 * LOG2_E
    nope = jnp.concatenate([jnp.ones((seq, A_NOPE), F32), zeros(LANES - A_NOPE)], axis=1)
    cos_q = (nope + cos_k) * scale_a
    sin_q = sin_k * scale_a
    quarter = C_HD // 4
    ang_r = _rope_angles(pos // GRID_W, quarter)
    ang_c = _rope_angles(pos % GRID_W, quarter)
    cos_c = jnp.concatenate([jnp.tile(jnp.cos(ang_r), (1, 2)), jnp.tile(jnp.cos(ang_c), (1, 2)), zeros(LANES - C_HD)], axis=1)
    sin_c = jnp.concatenate([jnp.tile(jnp.sin(ang_r), (1, 2)), jnp.tile(jnp.sin(ang_c), (1, 2)), zeros(LANES - C_HD)], axis=1)
    return jnp.stack([cos_q, sin_q, cos_k, sin_k, cos_c, sin_c], axis=0)


def _t5_bucket(rel):
    half = NUM_BUCKETS // 2
    max_exact = half // 2
    ret = jnp.where(rel > 0, half, 0)
    n = jnp.abs(rel)
    nf = jnp.maximum(n, 1).astype(F32)
    large = max_exact + (jnp.log(nf / max_exact) / math.log(MAX_DISTANCE / max_exact) * (half - max_exact)).astype(jnp.int32)
    large = jnp.minimum(large, half - 1)
    return ret + jnp.where(n < max_exact, n, large)


def _band_bias(t5_table, group, dilation):
    offs = dilation * jnp.arange(-B_SIDE, B_SIDE + 1, dtype=jnp.int32)
    bias_hj = t5_table[_t5_bucket(offs)][:, group * B_HEADS:(group + 1) * B_HEADS].T.astype(F32)
    rel = jnp.arange(B_WIN, dtype=jnp.int32)[None, :] - jnp.arange(B_TU, dtype=jnp.int32)[:, None]
    inside = (rel >= 0) & (rel <= 2 * B_SIDE)
    vals = bias_hj[:, jnp.clip(rel, 0, 2 * B_SIDE)]
    return jnp.where(inside[None], vals, NEG_BIG)


def _layer_weights(l, norm_mix, w_in, a_q_norm, a_kv_norm, a_w_uq, a_w_ukv, c_q_norm, c_k_norm,
                   w_br_a, w_br_b, w_br_c, w_out, norm_ffn, w_up, w_down):
    w = w_in[l]
    o = 0
    cols = []
    for width in (A_Q_LORA, A_KV_LORA, A_ROPE, 3 * B_GROUPS * B_HEADS * B_HD, C_HEADS * C_HD,
                  C_KV_HEADS * C_HD, C_KV_HEADS * C_HD, N_BRANCH * D_MODEL):
        cols.append(w[:, o:o + width])
        o += width
    w_cq, w_ckv, w_kr, w_b, w_qc, w_kc, w_vc, w_gate = cols
    kr_p = _pad_heads(w_kr, 1, A_ROPE, A_NOPE)
    kr_rot_p = _pad_heads(_rot_cols(w_kr), 1, A_ROPE, A_NOPE)
    qc = w_qc.reshape(D_MODEL, C_HEADS, C_HD)
    kc = w_kc.reshape(D_MODEL, C_KV_HEADS, C_HD)
    w1 = jnp.concatenate([
        w_cq, w_ckv, kr_p, kr_rot_p,
        _pad_heads(w_qc, C_HEADS, C_HD), _pad_heads(_rot_axial(qc).reshape(D_MODEL, -1), C_HEADS, C_HD),
        _pad_heads(w_kc, C_KV_HEADS, C_HD), _pad_heads(_rot_axial(kc).reshape(D_MODEL, -1), C_KV_HEADS, C_HD),
        _pad_heads(w_vc, C_KV_HEADS, C_HD)], axis=1).astype(BF16)
    uq = a_w_uq[l].reshape(A_Q_LORA, A_HEADS, A_NOPE + A_ROPE)
    uq_rot = jnp.concatenate([jnp.zeros((A_Q_LORA, A_HEADS, A_NOPE), F32), _rot_cols(uq[..., A_NOPE:])], axis=-1)
    wq2 = jnp.concatenate([_pad_heads(uq.reshape(A_Q_LORA, -1), A_HEADS, A_NOPE + A_ROPE),
                           _pad_heads(uq_rot.reshape(A_Q_LORA, -1), A_HEADS, A_NOPE + A_ROPE)], axis=1).astype(BF16)
    ukv = a_w_ukv[l].reshape(A_KV_LORA, A_HEADS, A_NOPE + A_V)
    wkv2 = jnp.concatenate([_pad_heads(ukv[..., :A_NOPE].reshape(A_KV_LORA, -1), A_HEADS, A_NOPE),
                            _pad_heads(ukv[..., A_NOPE:].reshape(A_KV_LORA, -1), A_HEADS, A_V)], axis=1).astype(BF16)
    pad_gain = lambda g: jnp.pad(g, (0, LANES - C_HD))
    scale_c = C_HD ** -0.5 * LOG2_E
    cg = jnp.stack([pad_gain(c_q_norm[l]) * scale_c, pad_gain(_swap_axial(c_q_norm[l])) * scale_c,
                    pad_gain(c_k_norm[l]), pad_gain(_swap_axial(c_k_norm[l]))], axis=0)
    wb5 = w_b.reshape(D_MODEL, 3, B_GROUPS, B_HEADS * B_HD)
    wb5 = wb5 * jnp.array([B_HD ** -0.5, 1.0, 1.0], F32)[None, :, None, None]
    w_bq = jnp.transpose(wb5, (0, 2, 1, 3)).reshape(D_MODEL, -1).astype(BF16)
    return dict(
        gmix=norm_mix[l][None, :], w1=w1, gq=a_q_norm[l][None, :], gkv=a_kv_norm[l][None, :], wq2=wq2, wkv2=wkv2,
        cg=cg, w_bq=w_bq, w_gate=w_gate.astype(BF16),
        wa=_pad_head_rows(w_br_a[l], A_HEADS, A_V).astype(BF16), wb=w_br_b[l].astype(BF16),
        wc=_pad_head_rows(w_br_c[l], C_HEADS, C_HD).astype(BF16), wo=w_out[l].astype(BF16),
        gffn=norm_ffn[l][None, :], wup=w_up[l].astype(BF16), wdn=w_down[l].astype(BF16))


def _encoder_layer(x, wts, tabs, biases, *, nb, seq, final, final_norm, tm, tq, tk):
    qa, ka, va, qc, kc, vc = _prep_ac(x, wts["gmix"], wts["w1"], wts["gq"], wts["gkv"], wts["wq2"], wts["wkv2"],
                                      wts["cg"], tabs, seq=seq, tm=tm)
    zb = _rms_matmul(x, wts["gmix"], wts["w_bq"], BF16, tm=tm, name="proj_b")
    gates = _rms_matmul(x, wts["gmix"], wts["w_gate"], F32, tm=tm, name="proj_gate")

    per_seq = lambda a: a.reshape(nb, seq, -1)
    oa = _flash(per_seq(qa), per_seq(ka), per_seq(va), rep=1, tq=tq, tk=tk, name="flash_a").reshape(nb * seq, -1)
    oc = _flash(per_seq(qc), per_seq(kc), per_seq(vc), rep=C_HEADS // C_KV_HEADS, tq=tq, tk=tk,
                name="flash_c").reshape(nb * seq, -1)

    zb3 = zb.reshape(nb, seq, -1)
    obs, lses = [], []
    for g, (_, dilation) in enumerate(B_PAIRS):
        o_g, lse_g = _dilated(zb3, biases[g], group=g, dilation=dilation, seq=seq, name=f"dilated_{g}")
        obs.append(o_g)
        lses.append(lse_g)

    x = _merge(x, oa, oc, obs, lses, gates, wts["wa"], wts["wb"], wts["wc"], wts["wo"], tm=tm)
    return _ffn(x, wts["gffn"], wts["wup"], wts["wdn"], final_norm, final=final, tm=tm)


def _trunk(x, norm_mix, w_in, a_q_norm, a_kv_norm, a_w_uq, a_w_ukv, c_q_norm, c_k_norm,
           w_br_a, w_br_b, w_br_c, w_out, norm_ffn, w_up, w_down, t5_table, final_norm, *, tm, tq, tk):
    nb, seq, _ = x.shape
    tabs = _tables(seq)
    biases = [_band_bias(t5_table, g, d) for g, (_, d) in enumerate(B_PAIRS)]
    depth = w_in.shape[0]
    xr = x.reshape(nb * seq, D_MODEL)
    for l in range(depth):
        wts = _layer_weights(l, norm_mix, w_in, a_q_norm, a_kv_norm, a_w_uq, a_w_ukv, c_q_norm, c_k_norm,
                             w_br_a, w_br_b, w_br_c, w_out, norm_ffn, w_up, w_down)
        xr = _encoder_layer(xr, wts, tabs, biases, nb=nb, seq=seq, final=(l == depth - 1),
                            final_norm=final_norm[None, :], tm=tm, tq=tq, tk=tk)
    return xr.reshape(nb, seq, D_MODEL)


def kernel(x_prompt, x_sample, norm_mix, w_in, a_q_norm, a_kv_norm, a_w_uq, a_w_ukv, c_q_norm, c_k_norm,
           w_br_a, w_br_b, w_br_c, w_out, norm_ffn, w_up, w_down, t5_table, final_norm):
    assert x_prompt.shape[1:] == x_sample.shape[1:]
    n_prompt = x_prompt.shape[0]
    x = jnp.concatenate([x_prompt, x_sample], axis=0)
    y = _trunk(x, norm_mix, w_in, a_q_norm, a_kv_norm, a_w_uq, a_w_ukv, c_q_norm, c_k_norm,
               w_br_a, w_br_b, w_br_c, w_out, norm_ffn, w_up, w_down, t5_table, final_norm,
               tm=256, tq=1024, tk=512)
    return (y[:n_prompt], y[n_prompt:])
```

```python
import functools
import math

import jax
import jax.numpy as jnp
from jax import lax
from jax.experimental import pallas as pl
from jax.experimental.pallas import tpu as pltpu

D_MODEL = 1024
GRID_W = 64
NORM_EPS = 1e-6
ROPE_THETA = 10000.0
NEG_BIG = -1e30
A_HEADS, A_NOPE, A_ROPE, A_V = 8, 64, 32, 64
A_Q_LORA, A_KV_LORA = 384, 256
B_PAIRS = ((128, 1), (512, 4), (2048, 16))
B_GROUPS, B_HEADS, B_HD = 3, 8, 64
C_HEADS, C_KV_HEADS, C_HD = 8, 2, 64
NUM_BUCKETS, MAX_DISTANCE = 32, 2048
D_FF = 4 * D_MODEL
N_BRANCH = 3
B_SIDE = 64

LANES = 128
VMEM_LIMIT = 48 * 1024 * 1024
FLASH_ROW_GROUP = 128
FLASH_SAFE_BITS = 64.0
FLASH_REF_OFFSET = 30.0

LOG2_E = math.log2(math.e)

BF16 = jnp.bfloat16
F32 = jnp.float32


def _params(*sem):
    return pltpu.CompilerParams(dimension_semantics=sem, vmem_limit_bytes=VMEM_LIMIT)


def _rms(x, g):
    return x * lax.rsqrt(jnp.mean(x * x, axis=-1, keepdims=True) + NORM_EPS) * g


_C_CQ = (0, 384)
_C_CKV = (384, 640)
_C_KR = (640, 768)
_C_KRR = (768, 896)
_C_QC = (896, 1920)
_C_QCR = (1920, 2944)
_C_KC = (2944, 3200)
_C_KCR = (3200, 3456)
_C_VC = (3456, 3712)


def _prep_ac_body(x_ref, gmix_ref, w1_ref, gq_ref, gkv_ref, wq2_ref, wkv2_ref, cg_ref, tab_ref,
                  qa_ref, ka_ref, va_ref, qc_ref, kc_ref, vc_ref):
    h = _rms(x_ref[...], gmix_ref[...]).astype(BF16)
    z = jnp.dot(h, w1_ref[...], preferred_element_type=F32)
    cqn = _rms(z[:, _C_CQ[0]:_C_CQ[1]], gq_ref[...]).astype(BF16)
    ckvn = _rms(z[:, _C_CKV[0]:_C_CKV[1]], gkv_ref[...]).astype(BF16)
    qq = jnp.dot(cqn, wq2_ref[...], preferred_element_type=F32)
    kv = jnp.dot(ckvn, wkv2_ref[...], preferred_element_type=F32)
    cos_q, sin_q, cos_k, sin_k, cos_c, sin_c = (tab_ref[t] for t in range(6))
    krope = z[:, _C_KR[0]:_C_KR[1]] * cos_k + z[:, _C_KRR[0]:_C_KRR[1]] * sin_k
    lane = lax.broadcasted_iota(jnp.int32, (1, LANES), 1)
    ones_col = (lane == A_V).astype(F32)
    ref_col = (lane == LANES - 1).astype(F32)
    hw = A_HEADS * LANES
    for hd in range(A_HEADS):
        sl = slice(hd * LANES, (hd + 1) * LANES)
        sr = slice(hw + hd * LANES, hw + (hd + 1) * LANES)
        qa_ref[:, sl] = (qq[:, sl] * cos_q + qq[:, sr] * sin_q).astype(BF16)
        ka_ref[:, sl] = (kv[:, sl] + krope + ref_col).astype(BF16)
        va_ref[:, sl] = (kv[:, sr] + ones_col).astype(BF16)
    gq_cos = cg_ref[0:1, :] * cos_c
    gq_sin = cg_ref[1:2, :] * sin_c
    gk_cos = cg_ref[2:3, :] * cos_c
    gk_sin = cg_ref[3:4, :] * sin_c
    for hd in range(C_HEADS):
        sl = slice(hd * LANES, (hd + 1) * LANES)
        q = z[:, _C_QC[0] + hd * LANES:_C_QC[0] + (hd + 1) * LANES]
        qr = z[:, _C_QCR[0] + hd * LANES:_C_QCR[0] + (hd + 1) * LANES]
        inv = lax.rsqrt(jnp.sum(q * q, axis=-1, keepdims=True) * (1.0 / C_HD) + NORM_EPS)
        qc_ref[:, sl] = ((q * gq_cos + qr * gq_sin) * inv).astype(BF16)
    for hd in range(C_KV_HEADS):
        sl = slice(hd * LANES, (hd + 1) * LANES)
        k = z[:, _C_KC[0] + hd * LANES:_C_KC[0] + (hd + 1) * LANES]
        kr = z[:, _C_KCR[0] + hd * LANES:_C_KCR[0] + (hd + 1) * LANES]
        inv = lax.rsqrt(jnp.sum(k * k, axis=-1, keepdims=True) * (1.0 / C_HD) + NORM_EPS)
        kc_ref[:, sl] = ((k * gk_cos + kr * gk_sin) * inv + ref_col).astype(BF16)
        vc_ref[:, sl] = (z[:, _C_VC[0] + hd * LANES:_C_VC[0] + (hd + 1) * LANES] + ones_col).astype(BF16)


def _prep_ac(x, gmix, w1, gq, gkv, wq2, wkv2, cg, tabs, *, seq, tm):
    rows = x.shape[0]
    tiles_per_seq = seq // tm
    full = lambda a: pl.BlockSpec(a.shape, lambda i: (0,) * a.ndim)
    row_spec = lambda w: pl.BlockSpec((tm, w), lambda i: (i, 0))
    wa, wc, wk = A_HEADS * LANES, C_HEADS * LANES, C_KV_HEADS * LANES
    return pl.pallas_call(
        _prep_ac_body,
        grid=(rows // tm,),
        in_specs=[row_spec(D_MODEL), full(gmix), full(w1), full(gq), full(gkv), full(wq2), full(wkv2), full(cg),
                  pl.BlockSpec((6, tm, LANES), lambda i: (0, i % tiles_per_seq, 0))],
        out_specs=[row_spec(wa), row_spec(wa), row_spec(wa), row_spec(wc), row_spec(wk), row_spec(wk)],
        out_shape=[jax.ShapeDtypeStruct((rows, w), BF16) for w in (wa, wa, wa, wc, wk, wk)],
        compiler_params=_params("arbitrary"),
        name="prep_ac",
    )(x, gmix, w1, gq, gkv, wq2, wkv2, cg, tabs)


def _rms_matmul_body(x_ref, g_ref, w_ref, o_ref):
    h = _rms(x_ref[...], g_ref[...]).astype(BF16)
    o_ref[...] = jnp.dot(h, w_ref[...], preferred_element_type=F32).astype(o_ref.dtype)


def _rms_matmul(x, g, w, out_dtype, *, tm, name):
    rows = x.shape[0]
    n = w.shape[1]
    return pl.pallas_call(
        _rms_matmul_body,
        grid=(rows // tm,),
        in_specs=[pl.BlockSpec((tm, D_MODEL), lambda i: (i, 0)),
                  pl.BlockSpec(g.shape, lambda i: (0, 0)),
                  pl.BlockSpec(w.shape, lambda i: (0, 0))],
        out_specs=pl.BlockSpec((tm, n), lambda i: (i, 0)),
        out_shape=jax.ShapeDtypeStruct((rows, n), out_dtype),
        compiler_params=_params("arbitrary"),
        name=name,
    )(x, g, w)


def _proj_b_body(x_ref, g_ref, w_ref, o0_ref, o1_ref, o2_ref, z_ref):
    h = _rms(x_ref[...], g_ref[...]).astype(BF16)
    z = jnp.dot(h, w_ref[...], preferred_element_type=F32)
    tm = x_ref.shape[0]
    n_lt = z_ref.shape[0]
    for c in range(n_lt):
        z_ref[c] = z[:, c * LANES:(c + 1) * LANES]
    lt_per_group = n_lt // B_GROUPS
    for g, o_ref in enumerate((o0_ref, o1_ref, o2_ref)):
        d = B_PAIRS[g][1]
        for r in range(d):
            for c in range(lt_per_group):
                piece = z_ref[g * lt_per_group + c, pl.ds(r, tm // d, stride=d), :]
                o_ref[r, :, c * LANES:(c + 1) * LANES] = piece.astype(o_ref.dtype)


def _proj_b(x, g, w, *, nb, seq, tm):
    rows = x.shape[0]
    tiles_per_seq = seq // tm
    gw = 3 * B_HEADS * B_HD
    out_specs, out_shapes = [], []
    for _, d in B_PAIRS:
        out_specs.append(pl.BlockSpec((None, d, tm // d, gw), lambda i: (i // tiles_per_seq, 0, i % tiles_per_seq, 0)))
        out_shapes.append(jax.ShapeDtypeStruct((nb, d, seq // d, gw), BF16))
    return pl.pallas_call(
        _proj_b_body,
        grid=(rows // tm,),
        in_specs=[pl.BlockSpec((tm, D_MODEL), lambda i: (i, 0)),
                  pl.BlockSpec(g.shape, lambda i: (0, 0)),
                  pl.BlockSpec(w.shape, lambda i: (0, 0))],
        out_specs=out_specs,
        out_shape=out_shapes,
        scratch_shapes=[pltpu.VMEM((w.shape[1] // LANES, tm, LANES), F32)],
        compiler_params=_params("arbitrary"),
        name="proj_b",
    )(x, g, w)


def _flash_body(q_ref, k_ref, v_ref, o_ref, qT_ref, qx_ref, vT_ref, ksq_ref, acc_ref, m_ref, s0_ref, s1_ref,
                mc0_ref, mc1_ref, p0_ref, p1_ref, al0_ref, al1_ref, *, tk, n_chunks):
    tq = q_ref.shape[0]
    kh = vT_ref.shape[2]
    assert tk == 2 * kh
    n_kh = vT_ref.shape[0]

    @pl.when(pl.program_id(2) == 0)
    def _():
        def tr(n, ksq):
            off = pl.multiple_of(n * kh, kh)
            vT_ref[n] = v_ref[pl.ds(off, kh), :].astype(F32).T.astype(BF16)
            kk = k_ref[pl.ds(off, kh), :].astype(F32)
            return jnp.maximum(ksq, jnp.sum(kk * kk, axis=1, keepdims=True))
        ksq = lax.fori_loop(0, n_kh, tr, jnp.zeros((kh, 1), F32))
        ksq_ref[...] = jnp.full(ksq_ref.shape, jnp.max(ksq), F32)

    qT = q_ref[...].astype(F32).T
    bound = jnp.sqrt(jnp.sum(qT * qT, axis=0, keepdims=True) * ksq_ref[...]) * 1.01 + 1.0
    safe = jnp.max(bound) <= FLASH_SAFE_BITS
    row = lax.broadcasted_iota(jnp.int32, (LANES, tq), 0)
    qT_ref[...] = qT.astype(BF16)
    qx_ref[...] = jnp.where(row == LANES - 1, FLASH_REF_OFFSET - bound, qT).astype(BF16)
    acc_ref[...] = jnp.zeros(acc_ref.shape, F32)

    def fast_scores(n, p_ref):
        off = pl.multiple_of(n * kh, kh)
        s = jnp.dot(k_ref[pl.ds(off, kh), :], qx_ref[...], preferred_element_type=F32)
        p_ref[0:kh, :] = jnp.exp2(s).astype(BF16)

    def fast_values(n, p_ref):
        acc_ref[...] = acc_ref[...] + jnp.dot(vT_ref[n], p_ref[0:kh, :], preferred_element_type=F32)

    @pl.when(safe)
    def _():
        fast_scores(0, p0_ref)

        def quad(j, carry):
            n0 = 4 * j
            fast_scores(n0 + 1, p1_ref)
            fast_values(n0, p0_ref)
            fast_scores(n0 + 2, p0_ref)
            fast_values(n0 + 1, p1_ref)
            fast_scores(n0 + 3, p1_ref)
            fast_values(n0 + 2, p0_ref)
            fast_scores(jnp.minimum(n0 + 4, n_kh - 1), p0_ref)
            fast_values(n0 + 3, p1_ref)
            return carry

        lax.fori_loop(0, n_kh // 4, quad, 0)

    @pl.when(jnp.logical_not(safe))
    def _():
        _flash_running_max(k_ref, qT_ref, vT_ref, acc_ref, m_ref, s0_ref, s1_ref, mc0_ref, mc1_ref,
                           p0_ref, p1_ref, al0_ref, al1_ref, tk=tk, n_chunks=n_chunks, tq=tq, kh=kh)

    acc = acc_ref[...]
    o_ref[...] = (acc * (1.0 / acc[A_V:A_V + 1, :])).T.astype(o_ref.dtype)


def _flash_running_max(k_ref, qT_ref, vT_ref, acc_ref, m_ref, s0_ref, s1_ref, mc0_ref, mc1_ref,
                       p0_ref, p1_ref, al0_ref, al1_ref, *, tk, n_chunks, tq, kh):
    rg = FLASH_ROW_GROUP
    n_groups = tk // rg
    m_ref[...] = jnp.full(m_ref.shape, NEG_BIG, F32)

    def stage_a(c, g, s_ref, mc_ref):
        off = pl.multiple_of(c * tk + g * rg, rg)
        sg = jnp.dot(k_ref[pl.ds(off, rg), :], qT_ref[...], preferred_element_type=F32)
        s_ref[g * rg:(g + 1) * rg, :] = sg
        part = jnp.max(sg.reshape(rg // 8, 8, tq), axis=0)
        mc_ref[...] = part if g == 0 else jnp.maximum(mc_ref[...], part)

    def stage_b_head(mc_ref, al_ref):
        m_prev = m_ref[...]
        m_new = jnp.maximum(m_prev, jnp.max(mc_ref[...], axis=0, keepdims=True))
        al_ref[...] = jnp.exp2(m_prev - m_new)
        m_ref[...] = m_new

    def stage_b(g, s_ref, p_ref):
        rows = slice(g * rg, (g + 1) * rg)
        p_ref[rows, :] = jnp.exp2(s_ref[rows, :] - m_ref[...]).astype(BF16)

    def stage_c(c, half, p_ref, al_ref):
        rows = slice(half * kh, (half + 1) * kh)
        pv = jnp.dot(vT_ref[2 * c + half], p_ref[rows, :], preferred_element_type=F32)
        if half == 0:
            acc_ref[...] = acc_ref[...] * al_ref[...] + pv
        else:
            acc_ref[...] = acc_ref[...] + pv

    def step(c_a, c_c, s_in, mc_in, s_out, mc_out, p_in, al_in, p_out, al_out):
        stage_b_head(mc_in, al_out)
        for g in range(n_groups):
            if g % (n_groups // 2) == 0:
                stage_c(c_c, g // (n_groups // 2), p_in, al_in)
            stage_a(c_a, g, s_out, mc_out)
            stage_b(g, s_in, p_out)

    for g in range(n_groups):
        stage_a(0, g, s0_ref, mc0_ref)
    stage_b_head(mc0_ref, al0_ref)
    for g in range(n_groups):
        stage_a(1, g, s1_ref, mc1_ref)
        stage_b(g, s0_ref, p0_ref)

    def pair(j, carry):
        c0 = 2 * j
        step(jnp.minimum(c0 + 2, n_chunks - 1), c0, s1_ref, mc1_ref, s0_ref, mc0_ref, p0_ref, al0_ref, p1_ref, al1_ref)
        step(jnp.minimum(c0 + 3, n_chunks - 1), c0 + 1, s0_ref, mc0_ref, s1_ref, mc1_ref, p1_ref, al1_ref, p0_ref, al0_ref)
        return carry

    lax.fori_loop(0, n_chunks // 2, pair, 0)


def _flash(q, k, v, *, rep, tq, tk, name):
    nb, seq, wq = q.shape
    hq = wq // LANES
    n_chunks = seq // tk
    kh = tk // 2
    assert n_chunks % 2 == 0 and (seq // kh) % 4 == 0
    body = functools.partial(_flash_body, tk=tk, n_chunks=n_chunks)
    return pl.pallas_call(
        body,
        grid=(nb, hq, seq // tq),
        in_specs=[pl.BlockSpec((None, tq, LANES), lambda b, h, i: (b, i, h)),
                  pl.BlockSpec((None, seq, LANES), lambda b, h, i: (b, 0, h // rep)),
                  pl.BlockSpec((None, seq, LANES), lambda b, h, i: (b, 0, h // rep))],
        out_specs=pl.BlockSpec((None, tq, LANES), lambda b, h, i: (b, i, h)),
        out_shape=jax.ShapeDtypeStruct((nb, seq, wq), BF16),
        scratch_shapes=[pltpu.VMEM((LANES, tq), BF16), pltpu.VMEM((LANES, tq), BF16),
                        pltpu.VMEM((seq // kh, LANES, kh), BF16), pltpu.VMEM((1, tq), F32),
                        pltpu.VMEM((LANES, tq), F32), pltpu.VMEM((1, tq), F32),
                        pltpu.VMEM((tk, tq), F32), pltpu.VMEM((tk, tq), F32),
                        pltpu.VMEM((8, tq), F32), pltpu.VMEM((8, tq), F32),
                        pltpu.VMEM((tk, tq), BF16), pltpu.VMEM((tk, tq), BF16),
                        pltpu.VMEM((1, tq), F32), pltpu.VMEM((1, tq), F32)],
        compiler_params=_params("arbitrary", "arbitrary", "arbitrary"),
        name=name,
    )(q, k, v)


B_TU = 128
B_WIN = B_TU + 2 * B_SIDE


def _dilated_body(q_ref, kp_ref, kc_ref, kn_ref, vp_ref, vc_ref, vn_ref, bias_ref, o_ref, lse_ref, *, n_u):
    u0 = pl.program_id(2) * B_TU
    kw = jnp.concatenate([kp_ref[B_TU - B_SIDE:, :], kc_ref[...], kn_ref[:B_SIDE, :]], axis=0)
    vw = jnp.concatenate([vp_ref[B_TU - B_SIDE:, :], vc_ref[...], vn_ref[:B_SIDE, :]], axis=0)
    key_u = u0 - B_SIDE + lax.broadcasted_iota(jnp.int32, (B_TU, B_WIN), 1)
    valid = (key_u >= 0) & (key_u < n_u)
    lane = lax.broadcasted_iota(jnp.int32, (B_TU, LANES), 1)
    low = lane < B_HD
    for hp in range(B_HEADS // 2):
        sl = slice(hp * LANES, (hp + 1) * LANES)
        qp = q_ref[:, sl]
        kp = kw[:, sl]
        vp = vw[:, sl]
        outs, lses = [], []
        for sub in range(2):
            qm = jnp.where(low if sub == 0 else jnp.logical_not(low), qp, jnp.zeros_like(qp))
            s = lax.dot_general(qm, kp, (((1,), (1,)), ((), ())), preferred_element_type=F32)
            s = jnp.where(valid, s + bias_ref[2 * hp + sub], NEG_BIG)
            m = jnp.max(s, axis=-1, keepdims=True)
            p = jnp.exp(s - m)
            l = jnp.sum(p, axis=-1, keepdims=True)
            o = jnp.dot(p.astype(BF16), vp, preferred_element_type=F32) * (1.0 / l)
            outs.append(o)
            lses.append(jnp.broadcast_to(m + jnp.log(l), (B_TU, LANES)))
        o_ref[:, sl] = jnp.where(low, outs[0], outs[1]).astype(o_ref.dtype)
        lse_ref[:, sl] = jnp.where(low, lses[0], lses[1])


def _dilated(zg, bias, *, name):
    nb, dilation, n_u, _ = zg.shape
    nblk = n_u // B_TU
    hw = B_HEADS * B_HD

    def spec(part, shift):
        def imap(b, r, i):
            return (b, r, jnp.clip(i + shift, 0, nblk - 1), part)
        return pl.BlockSpec((None, None, B_TU, hw), imap)

    out_spec = pl.BlockSpec((None, None, B_TU, hw), lambda b, r, i: (b, r, i, 0))
    return pl.pallas_call(
        functools.partial(_dilated_body, n_u=n_u),
        grid=(nb, dilation, nblk),
        in_specs=[spec(0, 0), spec(1, -1), spec(1, 0), spec(1, 1), spec(2, -1), spec(2, 0), spec(2, 1),
                  pl.BlockSpec(bias.shape, lambda b, r, i: (0, 0, 0))],
        out_specs=[out_spec, out_spec],
        out_shape=[jax.ShapeDtypeStruct((nb, dilation, n_u, hw), BF16),
                   jax.ShapeDtypeStruct((nb, dilation, n_u, hw), F32)],
        compiler_params=_params("arbitrary", "arbitrary", "arbitrary"),
        name=name,
    )(zg, zg, zg, zg, zg, zg, zg, bias)


def _merge_body(x_ref, oa_ref, oc_ref, ob0_ref, ob1_ref, ob2_ref, l0_ref, l1_ref, l2_ref, gate_ref,
                wa_ref, wb_ref, wc_ref, wo_ref, out_ref, on_ref, ln_ref):
    tm = x_ref.shape[0]
    n_lt = on_ref.shape[1]
    for g, (o_ref, l_ref) in enumerate(((ob0_ref, l0_ref), (ob1_ref, l1_ref), (ob2_ref, l2_ref))):
        d = B_PAIRS[g][1]
        for r in range(d):
            for c in range(n_lt):
                cols = slice(c * LANES, (c + 1) * LANES)
                on_ref[g, c, pl.ds(r, tm // d, stride=d), :] = o_ref[r, :, cols].astype(F32)
                ln_ref[g, c, pl.ds(r, tm // d, stride=d), :] = l_ref[r, :, cols]
    wide = lambda ref, g: jnp.concatenate([ref[g, c] for c in range(n_lt)], axis=1)
    l0, l1, l2 = wide(ln_ref, 0), wide(ln_ref, 1), wide(ln_ref, 2)
    mx = jnp.maximum(jnp.maximum(l0, l1), l2)
    e0, e1, e2 = jnp.exp(l0 - mx), jnp.exp(l1 - mx), jnp.exp(l2 - mx)
    ob = e0 * wide(on_ref, 0) + e1 * wide(on_ref, 1) + e2 * wide(on_ref, 2)
    ob = (ob * (1.0 / (e0 + e1 + e2))).astype(BF16)
    sg = 1.0 / (1.0 + jnp.exp(-gate_ref[...]))
    mix = sg[:, 0:D_MODEL] * jnp.dot(oa_ref[...], wa_ref[...], preferred_element_type=F32)
    mix += sg[:, D_MODEL:2 * D_MODEL] * jnp.dot(ob, wb_ref[...], preferred_element_type=F32)
    mix += sg[:, 2 * D_MODEL:] * jnp.dot(oc_ref[...], wc_ref[...], preferred_element_type=F32)
    out_ref[...] = x_ref[...] + jnp.dot(mix.astype(BF16), wo_ref[...], preferred_element_type=F32)


def _merge(x, oa, oc, obs, lses, gates, wa, wb, wc, wo, *, seq, tm):
    rows = x.shape[0]
    tiles_per_seq = seq // tm
    row_spec = lambda w: pl.BlockSpec((tm, w), lambda i: (i, 0))
    full = lambda a: pl.BlockSpec(a.shape, lambda i: (0, 0))
    hw = B_HEADS * B_HD
    res_specs = [pl.BlockSpec((None, d, tm // d, hw), lambda i: (i // tiles_per_seq, 0, i % tiles_per_seq, 0))
                 for _, d in B_PAIRS]
    return pl.pallas_call(
        _merge_body,
        grid=(rows // tm,),
        in_specs=[row_spec(D_MODEL), row_spec(oa.shape[1]), row_spec(oc.shape[1])] + res_specs + res_specs
                 + [row_spec(N_BRANCH * D_MODEL)]
                 + [full(wa), full(wb), full(wc), full(wo)],
        out_specs=row_spec(D_MODEL),
        out_shape=jax.ShapeDtypeStruct((rows, D_MODEL), F32),
        scratch_shapes=[pltpu.VMEM((B_GROUPS, hw // LANES, tm, LANES), F32),
                        pltpu.VMEM((B_GROUPS, hw // LANES, tm, LANES), F32)],
        compiler_params=_params("arbitrary"),
        name="merge_out",
    )(x, oa, oc, *obs, *lses, gates, wa, wb, wc, wo)


FFN_CHUNK = 1024


def _ffn_body(x_ref, g_ref, wup_ref, wdn_ref, gfin_ref, out_ref, *, final):
    x = x_ref[...]
    h = _rms(x, g_ref[...]).astype(BF16)
    acc = x
    for c in range(D_FF // FFN_CHUNK):
        cs = slice(c * FFN_CHUNK, (c + 1) * FFN_CHUNK)
        u = jnp.dot(h, wup_ref[:, cs], preferred_element_type=F32)
        a = jnp.square(jnp.maximum(u, 0.0)).astype(BF16)
        acc = acc + jnp.dot(a, wdn_ref[cs, :], preferred_element_type=F32)
    if final:
        acc = _rms(acc, gfin_ref[...])
    out_ref[...] = acc


def _ffn(x, g, wup, wdn, gfin, *, final, tm):
    rows = x.shape[0]
    full = lambda a: pl.BlockSpec(a.shape, lambda i: (0, 0))
    return pl.pallas_call(
        functools.partial(_ffn_body, final=final),
        grid=(rows // tm,),
        in_specs=[pl.BlockSpec((tm, D_MODEL), lambda i: (i, 0)), full(g), full(wup), full(wdn), full(gfin)],
        out_specs=pl.BlockSpec((tm, D_MODEL), lambda i: (i, 0)),
        out_shape=jax.ShapeDtypeStruct((rows, D_MODEL), F32),
        compiler_params=_params("arbitrary"),
        name="ffn_final" if final else "ffn",
    )(x, g, wup, wdn, gfin)


def _pad_heads(w, n_heads, hd, lane_off=0):
    k = w.shape[0]
    w = w.reshape(k, n_heads, hd)
    w = jnp.pad(w, ((0, 0), (0, 0), (lane_off, LANES - lane_off - hd)))
    return w.reshape(k, n_heads * LANES)


def _pad_head_rows(w, n_heads, hd):
    n = w.shape[1]
    return jnp.pad(w.reshape(n_heads, hd, n), ((0, 0), (0, LANES - hd), (0, 0))).reshape(n_heads * LANES, n)


def _rot_cols(w):
    half = w.shape[-1] // 2
    return jnp.concatenate([-w[..., half:], w[..., :half]], axis=-1)


def _rot_axial(w):
    half = C_HD // 2
    return jnp.concatenate([_rot_cols(w[..., :half]), _rot_cols(w[..., half:])], axis=-1)


def _swap_axial(g):
    q = C_HD // 4
    return jnp.concatenate([g[q:2 * q], g[:q], g[3 * q:], g[2 * q:3 * q]], axis=-1)


def _rope_angles(pos, half):
    freqs = ROPE_THETA ** (-jnp.arange(half, dtype=F32) / half)
    return pos.astype(F32)[:, None] * freqs[None, :]


def _tables(seq):
    pos = jnp.arange(seq, dtype=jnp.int32)
    ang = _rope_angles(pos, A_ROPE // 2)
    cos_r = jnp.tile(jnp.cos(ang), (1, 2))
    sin_r = jnp.tile(jnp.sin(ang), (1, 2))
    zeros = lambda w: jnp.zeros((seq, w), F32)
    cos_k = jnp.concatenate([zeros(A_NOPE), cos_r, zeros(LANES - A_NOPE - A_ROPE)], axis=1)
    sin_k = jnp.concatenate([zeros(A_NOPE), sin_r, zeros(LANES - A_NOPE - A_ROPE)], axis=1)
    scale_a = (A_NOPE + A_ROPE) ** -0.5 * LOG2_E
    nope = jnp.concatenate([jnp.ones((seq, A_NOPE), F32), zeros(LANES - A_NOPE)], axis=1)
    cos_q = (nope + cos_k) * scale_a
    sin_q = sin_k * scale_a
    quarter = C_HD // 4
    ang_r = _rope_angles(pos // GRID_W, quarter)
    ang_c = _rope_angles(pos % GRID_W, quarter)
    cos_c = jnp.concatenate([jnp.tile(jnp.cos(ang_r), (1, 2)), jnp.tile(jnp.cos(ang_c), (1, 2)), zeros(LANES - C_HD)], axis=1)
    sin_c = jnp.concatenate([jnp.tile(jnp.sin(ang_r), (1, 2)), jnp.tile(jnp.sin(ang_c), (1, 2)), zeros(LANES - C_HD)], axis=1)
    return jnp.stack([cos_q, sin_q, cos_k, sin_k, cos_c, sin_c], axis=0)


def _t5_bucket(rel):
    half = NUM_BUCKETS // 2
    max_exact = half // 2
    ret = jnp.where(rel > 0, half, 0)
    n = jnp.abs(rel)
    nf = jnp.maximum(n, 1).astype(F32)
    large = max_exact + (jnp.log(nf / max_exact) / math.log(MAX_DISTANCE / max_exact) * (half - max_exact)).astype(jnp.int32)
    large = jnp.minimum(large, half - 1)
    return ret + jnp.where(n < max_exact, n, large)


def _band_bias(t5_table, group, dilation):
    offs = dilation * jnp.arange(-B_SIDE, B_SIDE + 1, dtype=jnp.int32)
    bias_hj = t5_table[_t5_bucket(offs)][:, group * B_HEADS:(group + 1) * B_HEADS].T.astype(F32)
    rel = jnp.arange(B_WIN, dtype=jnp.int32)[None, :] - jnp.arange(B_TU, dtype=jnp.int32)[:, None]
    inside = (rel >= 0) & (rel <= 2 * B_SIDE)
    vals = bias_hj[:, jnp.clip(rel, 0, 2 * B_SIDE)]
    return jnp.where(inside[None], vals, NEG_BIG)


def _layer_weights(l, norm_mix, w_in, a_q_norm, a_kv_norm, a_w_uq, a_w_ukv, c_q_norm, c_k_norm,
                   w_br_a, w_br_b, w_br_c, w_out, norm_ffn, w_up, w_down):
    w = w_in[l]
    o = 0
    cols = []
    for width in (A_Q_LORA, A_KV_LORA, A_ROPE, 3 * B_GROUPS * B_HEADS * B_HD, C_HEADS * C_HD,
                  C_KV_HEADS * C_HD, C_KV_HEADS * C_HD, N_BRANCH * D_MODEL):
        cols.append(w[:, o:o + width])
        o += width
    w_cq, w_ckv, w_kr, w_b, w_qc, w_kc, w_vc, w_gate = cols
    kr_p = _pad_heads(w_kr, 1, A_ROPE, A_NOPE)
    kr_rot_p = _pad_heads(_rot_cols(w_kr), 1, A_ROPE, A_NOPE)
    qc = w_qc.reshape(D_MODEL, C_HEADS, C_HD)
    kc = w_kc.reshape(D_MODEL, C_KV_HEADS, C_HD)
    w1 = jnp.concatenate([
        w_cq, w_ckv, kr_p, kr_rot_p,
        _pad_heads(w_qc, C_HEADS, C_HD), _pad_heads(_rot_axial(qc).reshape(D_MODEL, -1), C_HEADS, C_HD),
        _pad_heads(w_kc, C_KV_HEADS, C_HD), _pad_heads(_rot_axial(kc).reshape(D_MODEL, -1), C_KV_HEADS, C_HD),
        _pad_heads(w_vc, C_KV_HEADS, C_HD)], axis=1).astype(BF16)
    uq = a_w_uq[l].reshape(A_Q_LORA, A_HEADS, A_NOPE + A_ROPE)
    uq_rot = jnp.concatenate([jnp.zeros((A_Q_LORA, A_HEADS, A_NOPE), F32), _rot_cols(uq[..., A_NOPE:])], axis=-1)
    wq2 = jnp.concatenate([_pad_heads(uq.reshape(A_Q_LORA, -1), A_HEADS, A_NOPE + A_ROPE),
                           _pad_heads(uq_rot.reshape(A_Q_LORA, -1), A_HEADS, A_NOPE + A_ROPE)], axis=1).astype(BF16)
    ukv = a_w_ukv[l].reshape(A_KV_LORA, A_HEADS, A_NOPE + A_V)
    wkv2 = jnp.concatenate([_pad_heads(ukv[..., :A_NOPE].reshape(A_KV_LORA, -1), A_HEADS, A_NOPE),
                            _pad_heads(ukv[..., A_NOPE:].reshape(A_KV_LORA, -1), A_HEADS, A_V)], axis=1).astype(BF16)
    pad_gain = lambda g: jnp.pad(g, (0, LANES - C_HD))
    scale_c = C_HD ** -0.5 * LOG2_E
    cg = jnp.stack([pad_gain(c_q_norm[l]) * scale_c, pad_gain(_swap_axial(c_q_norm[l])) * scale_c,
                    pad_gain(c_k_norm[l]), pad_gain(_swap_axial(c_k_norm[l]))], axis=0)
    wb5 = w_b.reshape(D_MODEL, 3, B_GROUPS, B_HEADS * B_HD)
    wb5 = wb5 * jnp.array([B_HD ** -0.5, 1.0, 1.0], F32)[None, :, None, None]
    w_bq = jnp.transpose(wb5, (0, 2, 1, 3)).reshape(D_MODEL, -1).astype(BF16)
    return dict(
        gmix=norm_mix[l][None, :], w1=w1, gq=a_q_norm[l][None, :], gkv=a_kv_norm[l][None, :], wq2=wq2, wkv2=wkv2,
        cg=cg, w_bq=w_bq, w_gate=w_gate.astype(BF16),
        wa=_pad_head_rows(w_br_a[l], A_HEADS, A_V).astype(BF16), wb=w_br_b[l].astype(BF16),
        wc=_pad_head_rows(w_br_c[l], C_HEADS, C_HD).astype(BF16), wo=w_out[l].astype(BF16),
        gffn=norm_ffn[l][None, :], wup=w_up[l].astype(BF16), wdn=w_down[l].astype(BF16))


def _encoder_layer(x, wts, tabs, biases, *, nb, seq, final, final_norm, tm, tq, tk):
    qa, ka, va, qc, kc, vc = _prep_ac(x, wts["gmix"], wts["w1"], wts["gq"], wts["gkv"], wts["wq2"], wts["wkv2"],
                                      wts["cg"], tabs, seq=seq, tm=tm)
    zbs = _proj_b(x, wts["gmix"], wts["w_bq"], nb=nb, seq=seq, tm=tm)
    gates = _rms_matmul(x, wts["gmix"], wts["w_gate"], F32, tm=tm, name="proj_gate")

    per_seq = lambda a: a.reshape(nb, seq, -1)
    oa = _flash(per_seq(qa), per_seq(ka), per_seq(va), rep=1, tq=tq, tk=tk, name="flash_a").reshape(nb * seq, -1)
    oc = _flash(per_seq(qc), per_seq(kc), per_seq(vc), rep=C_HEADS // C_KV_HEADS, tq=tq, tk=tk,
                name="flash_c").reshape(nb * seq, -1)

    obs, lses = [], []
    for g in range(B_GROUPS):
        o_g, lse_g = _dilated(zbs[g], biases[g], name=f"dilated_{g}")
        obs.append(o_g)
        lses.append(lse_g)

    x = _merge(x, oa, oc, obs, lses, gates, wts["wa"], wts["wb"], wts["wc"], wts["wo"], seq=seq, tm=tm)
    return _ffn(x, wts["gffn"], wts["wup"], wts["wdn"], final_norm, final=final, tm=tm)


def _trunk(x, norm_mix, w_in, a_q_norm, a_kv_norm, a_w_uq, a_w_ukv, c_q_norm, c_k_norm,
           w_br_a, w_br_b, w_br_c, w_out, norm_ffn, w_up, w_down, t5_table, final_norm, *, tm, tq, tk):
    nb, seq, _ = x.shape
    tabs = _tables(seq)
    biases = [_band_bias(t5_table, g, d) for g, (_, d) in enumerate(B_PAIRS)]
    depth = w_in.shape[0]
    xr = x.reshape(nb * seq, D_MODEL)
    for l in range(depth):
        wts = _layer_weights(l, norm_mix, w_in, a_q_norm, a_kv_norm, a_w_uq, a_w_ukv, c_q_norm, c_k_norm,
                             w_br_a, w_br_b, w_br_c, w_out, norm_ffn, w_up, w_down)
        xr = _encoder_layer(xr, wts, tabs, biases, nb=nb, seq=seq, final=(l == depth - 1),
                            final_norm=final_norm[None, :], tm=tm, tq=tq, tk=tk)
    return xr.reshape(nb, seq, D_MODEL)


def kernel(x_prompt, x_sample, norm_mix, w_in, a_q_norm, a_kv_norm, a_w_uq, a_w_ukv, c_q_norm, c_k_norm,
           w_br_a, w_br_b, w_br_c, w_out, norm_ffn, w_up, w_down, t5_table, final_norm):
    assert x_prompt.shape[1:] == x_sample.shape[1:]
    n_prompt = x_prompt.shape[0]
    x = jnp.concatenate([x_prompt, x_sample], axis=0)
    y = _trunk(x, norm_mix, w_in, a_q_norm, a_kv_norm, a_w_uq, a_w_ukv, c_q_norm, c_k_norm,
               w_br_a, w_br_b, w_br_c, w_out, norm_ffn, w_up, w_down, t5_table, final_norm,
               tm=256, tq=1024, tk=1024)
    return (y[:n_prompt], y[n_prompt:])
```

```python
import functools
import math

import jax
import jax.numpy as jnp
from jax import lax
from jax.experimental import pallas as pl
from jax.experimental.pallas import tpu as pltpu

D_MODEL = 1024
GRID_W = 64
NORM_EPS = 1e-6
ROPE_THETA = 10000.0
NEG_BIG = -1e30
A_HEADS, A_NOPE, A_ROPE, A_V = 8, 64, 32, 64
A_Q_LORA, A_KV_LORA = 384, 256
B_PAIRS = ((128, 1), (512, 4), (2048, 16))
B_GROUPS, B_HEADS, B_HD = 3, 8, 64
C_HEADS, C_KV_HEADS, C_HD = 8, 2, 64
NUM_BUCKETS, MAX_DISTANCE = 32, 2048
D_FF = 4 * D_MODEL
N_BRANCH = 3
B_SIDE = 64

LANES = 128
VMEM_LIMIT = 48 * 1024 * 1024
FLASH_ROW_GROUP = 128
FLASH_SAFE_BITS = 64.0
FLASH_REF_OFFSET = 30.0

LOG2_E = math.log2(math.e)

BF16 = jnp.bfloat16
F32 = jnp.float32


def _params(*sem):
    return pltpu.CompilerParams(dimension_semantics=sem, vmem_limit_bytes=VMEM_LIMIT)


def _resident(a):
    return pl.BlockSpec(a.shape, lambda i: (0,) * a.ndim, pipeline_mode=pl.Buffered(1))


def _rms(x, g):
    return x * lax.rsqrt(jnp.mean(x * x, axis=-1, keepdims=True) + NORM_EPS) * g


_C_CQ = (0, 384)
_C_CKV = (384, 640)
_C_KR = (640, 768)
_C_KRR = (768, 896)
_C_QC = (896, 1920)
_C_QCR = (1920, 2944)
_C_KC = (2944, 3200)
_C_KCR = (3200, 3456)
_C_VC = (3456, 3712)


def _prep_ac_body(x_ref, gmix_ref, w1_ref, gq_ref, gkv_ref, wq2_ref, wkv2_ref, cg_ref, tab_ref,
                  qa_ref, ka_ref, va_ref, qc_ref, kc_ref, vc_ref):
    h = _rms(x_ref[...], gmix_ref[...]).astype(BF16)
    z = jnp.dot(h, w1_ref[...], preferred_element_type=F32)
    cqn = _rms(z[:, _C_CQ[0]:_C_CQ[1]], gq_ref[...]).astype(BF16)
    ckvn = _rms(z[:, _C_CKV[0]:_C_CKV[1]], gkv_ref[...]).astype(BF16)
    qq = jnp.dot(cqn, wq2_ref[...], preferred_element_type=F32)
    kv = jnp.dot(ckvn, wkv2_ref[...], preferred_element_type=F32)
    cos_q, sin_q, cos_k, sin_k, cos_c, sin_c = (tab_ref[t] for t in range(6))
    krope = z[:, _C_KR[0]:_C_KR[1]] * cos_k + z[:, _C_KRR[0]:_C_KRR[1]] * sin_k
    lane = lax.broadcasted_iota(jnp.int32, (1, LANES), 1)
    ones_col = (lane == A_V).astype(F32)
    ref_col = (lane == LANES - 1).astype(F32)
    hw = A_HEADS * LANES
    for hd in range(A_HEADS):
        sl = slice(hd * LANES, (hd + 1) * LANES)
        sr = slice(hw + hd * LANES, hw + (hd + 1) * LANES)
        qa_ref[:, sl] = (qq[:, sl] * cos_q + qq[:, sr] * sin_q).astype(BF16)
        ka_ref[:, sl] = (kv[:, sl] + krope + ref_col).astype(BF16)
        va_ref[:, sl] = (kv[:, sr] + ones_col).astype(BF16)
    gq_cos = cg_ref[0:1, :] * cos_c
    gq_sin = cg_ref[1:2, :] * sin_c
    gk_cos = cg_ref[2:3, :] * cos_c
    gk_sin = cg_ref[3:4, :] * sin_c
    for hd in range(C_HEADS):
        sl = slice(hd * LANES, (hd + 1) * LANES)
        q = z[:, _C_QC[0] + hd * LANES:_C_QC[0] + (hd + 1) * LANES]
        qr = z[:, _C_QCR[0] + hd * LANES:_C_QCR[0] + (hd + 1) * LANES]
        inv = lax.rsqrt(jnp.sum(q * q, axis=-1, keepdims=True) * (1.0 / C_HD) + NORM_EPS)
        qc_ref[:, sl] = ((q * gq_cos + qr * gq_sin) * inv).astype(BF16)
    for hd in range(C_KV_HEADS):
        sl = slice(hd * LANES, (hd + 1) * LANES)
        k = z[:, _C_KC[0] + hd * LANES:_C_KC[0] + (hd + 1) * LANES]
        kr = z[:, _C_KCR[0] + hd * LANES:_C_KCR[0] + (hd + 1) * LANES]
        inv = lax.rsqrt(jnp.sum(k * k, axis=-1, keepdims=True) * (1.0 / C_HD) + NORM_EPS)
        kc_ref[:, sl] = ((k * gk_cos + kr * gk_sin) * inv + ref_col).astype(BF16)
        vc_ref[:, sl] = (z[:, _C_VC[0] + hd * LANES:_C_VC[0] + (hd + 1) * LANES] + ones_col).astype(BF16)


def _prep_ac(x, gmix, w1, gq, gkv, wq2, wkv2, cg, tabs, *, seq, tm):
    rows = x.shape[0]
    tiles_per_seq = seq // tm
    full = _resident
    row_spec = lambda w: pl.BlockSpec((tm, w), lambda i: (i, 0))
    wa, wc, wk = A_HEADS * LANES, C_HEADS * LANES, C_KV_HEADS * LANES
    return pl.pallas_call(
        _prep_ac_body,
        grid=(rows // tm,),
        in_specs=[row_spec(D_MODEL), full(gmix), full(w1), full(gq), full(gkv), full(wq2), full(wkv2), full(cg),
                  pl.BlockSpec((6, tm, LANES), lambda i: (0, i % tiles_per_seq, 0))],
        out_specs=[row_spec(wa), row_spec(wa), row_spec(wa), row_spec(wc), row_spec(wk), row_spec(wk)],
        out_shape=[jax.ShapeDtypeStruct((rows, w), BF16) for w in (wa, wa, wa, wc, wk, wk)],
        compiler_params=_params("arbitrary"),
        name="prep_ac",
    )(x, gmix, w1, gq, gkv, wq2, wkv2, cg, tabs)


def _proj_b_body(x_ref, g_ref, w_ref, o0_ref, o1_ref, o2_ref, z_ref):
    h = _rms(x_ref[...], g_ref[...]).astype(BF16)
    z = jnp.dot(h, w_ref[...], preferred_element_type=F32)
    tm = x_ref.shape[0]
    n_lt = z_ref.shape[0]
    for c in range(n_lt):
        z_ref[c] = z[:, c * LANES:(c + 1) * LANES]
    lt_per_group = n_lt // B_GROUPS
    for g, o_ref in enumerate((o0_ref, o1_ref, o2_ref)):
        d = B_PAIRS[g][1]
        for r in range(d):
            for c in range(lt_per_group):
                piece = z_ref[g * lt_per_group + c, pl.ds(r, tm // d, stride=d), :]
                o_ref[r, :, c * LANES:(c + 1) * LANES] = piece.astype(o_ref.dtype)


def _proj_b(x, g, w, *, nb, seq, tm):
    rows = x.shape[0]
    tiles_per_seq = seq // tm
    gw = 3 * B_HEADS * B_HD
    out_specs, out_shapes = [], []
    for _, d in B_PAIRS:
        out_specs.append(pl.BlockSpec((None, d, tm // d, gw), lambda i: (i // tiles_per_seq, 0, i % tiles_per_seq, 0)))
        out_shapes.append(jax.ShapeDtypeStruct((nb, d, seq // d, gw), BF16))
    return pl.pallas_call(
        _proj_b_body,
        grid=(rows // tm,),
        in_specs=[pl.BlockSpec((tm, D_MODEL), lambda i: (i, 0)), _resident(g), _resident(w)],
        out_specs=out_specs,
        out_shape=out_shapes,
        scratch_shapes=[pltpu.VMEM((w.shape[1] // LANES, tm, LANES), F32)],
        compiler_params=_params("arbitrary"),
        name="proj_b",
    )(x, g, w)


def _flash_body(q_ref, k_ref, v_ref, o_ref, qT_ref, qx_ref, vT_ref, ksq_ref, acc_ref, m_ref, s0_ref, s1_ref,
                mc0_ref, mc1_ref, p0_ref, p1_ref, al0_ref, al1_ref, *, tk, n_chunks):
    tq = q_ref.shape[0]
    kh = vT_ref.shape[2]
    assert tk == kh
    n_kh = vT_ref.shape[0]

    @pl.when(pl.program_id(2) == 0)
    def _():
        def tr(n, ksq):
            off = pl.multiple_of(n * kh, kh)
            vT_ref[n] = v_ref[pl.ds(off, kh), :].astype(F32).T.astype(BF16)
            kk = k_ref[pl.ds(off, kh), :].astype(F32)
            return jnp.maximum(ksq, jnp.sum(kk * kk, axis=1, keepdims=True))
        ksq = lax.fori_loop(0, n_kh, tr, jnp.zeros((kh, 1), F32))
        ksq_ref[...] = jnp.full(ksq_ref.shape, jnp.max(ksq), F32)

    qT = q_ref[...].astype(F32).T
    bound = jnp.sqrt(jnp.sum(qT * qT, axis=0, keepdims=True) * ksq_ref[...]) * 1.01 + 1.0
    safe = jnp.max(bound) <= FLASH_SAFE_BITS
    row = lax.broadcasted_iota(jnp.int32, (LANES, tq), 0)
    qT_ref[...] = qT.astype(BF16)
    qx_ref[...] = jnp.where(row == LANES - 1, FLASH_REF_OFFSET - bound, qT).astype(BF16)
    acc_ref[...] = jnp.zeros(acc_ref.shape, F32)

    def fast_scores(n, p_ref):
        off = pl.multiple_of(n * kh, kh)
        s = jnp.dot(k_ref[pl.ds(off, kh), :], qx_ref[...], preferred_element_type=F32)
        p_ref[...] = jnp.exp2(s).astype(BF16)

    def fast_values(n, p_ref):
        acc_ref[...] = acc_ref[...] + jnp.dot(vT_ref[n], p_ref[...], preferred_element_type=F32)

    @pl.when(safe)
    def _():
        fast_scores(0, p0_ref)

        def quad(n0, last):
            fast_scores(n0 + 1, p1_ref)
            fast_values(n0, p0_ref)
            fast_scores(n0 + 2, p0_ref)
            fast_values(n0 + 1, p1_ref)
            fast_scores(n0 + 3, p1_ref)
            fast_values(n0 + 2, p0_ref)
            if not last:
                fast_scores(n0 + 4, p0_ref)
            fast_values(n0 + 3, p1_ref)

        def body(j, carry):
            quad(4 * j, False)
            return carry

        lax.fori_loop(0, n_kh // 4 - 1, body, 0)
        quad(n_kh - 4, True)

    @pl.when(jnp.logical_not(safe))
    def _():
        _flash_running_max(k_ref, qT_ref, vT_ref, acc_ref, m_ref, s0_ref, s1_ref, mc0_ref, mc1_ref,
                           p0_ref, p1_ref, al0_ref, al1_ref, tk=tk, n_chunks=n_chunks, tq=tq)

    acc = acc_ref[...]
    o_ref[...] = (acc * (1.0 / acc[A_V:A_V + 1, :])).T.astype(o_ref.dtype)


def _flash_running_max(k_ref, qT_ref, vT_ref, acc_ref, m_ref, s0_ref, s1_ref, mc0_ref, mc1_ref,
                       p0_ref, p1_ref, al0_ref, al1_ref, *, tk, n_chunks, tq):
    rg = FLASH_ROW_GROUP
    n_groups = tk // rg
    m_ref[...] = jnp.full(m_ref.shape, NEG_BIG, F32)

    def stage_a(c, g, s_ref, mc_ref):
        off = pl.multiple_of(c * tk + g * rg, rg)
        sg = jnp.dot(k_ref[pl.ds(off, rg), :], qT_ref[...], preferred_element_type=F32)
        s_ref[g * rg:(g + 1) * rg, :] = sg
        part = jnp.max(sg.reshape(rg // 8, 8, tq), axis=0)
        mc_ref[...] = part if g == 0 else jnp.maximum(mc_ref[...], part)

    def stage_b_head(mc_ref, al_ref):
        m_prev = m_ref[...]
        m_new = jnp.maximum(m_prev, jnp.max(mc_ref[...], axis=0, keepdims=True))
        al_ref[...] = jnp.exp2(m_prev - m_new)
        m_ref[...] = m_new

    def stage_b(g, s_ref, p_ref):
        rows = slice(g * rg, (g + 1) * rg)
        p_ref[rows, :] = jnp.exp2(s_ref[rows, :] - m_ref[...]).astype(BF16)

    def stage_c(c, p_ref, al_ref):
        pv = jnp.dot(vT_ref[c], p_ref[...], preferred_element_type=F32)
        acc_ref[...] = acc_ref[...] * al_ref[...] + pv

    def step(c_a, c_c, s_in, mc_in, s_out, mc_out, p_in, al_in, p_out, al_out):
        stage_b_head(mc_in, al_out)
        stage_c(c_c, p_in, al_in)
        for g in range(n_groups):
            stage_a(c_a, g, s_out, mc_out)
            stage_b(g, s_in, p_out)

    for g in range(n_groups):
        stage_a(0, g, s0_ref, mc0_ref)
    stage_b_head(mc0_ref, al0_ref)
    for g in range(n_groups):
        stage_a(1, g, s1_ref, mc1_ref)
        stage_b(g, s0_ref, p0_ref)

    def pair(j, carry):
        c0 = 2 * j
        step(jnp.minimum(c0 + 2, n_chunks - 1), c0, s1_ref, mc1_ref, s0_ref, mc0_ref, p0_ref, al0_ref, p1_ref, al1_ref)
        step(jnp.minimum(c0 + 3, n_chunks - 1), c0 + 1, s0_ref, mc0_ref, s1_ref, mc1_ref, p1_ref, al1_ref, p0_ref, al0_ref)
        return carry

    lax.fori_loop(0, n_chunks // 2, pair, 0)


def _flash(q, k, v, *, rep, tq, tk, name):
    nb, seq, wq = q.shape
    hq = wq // LANES
    n_chunks = seq // tk
    kh = tk
    assert n_chunks % 4 == 0
    body = functools.partial(_flash_body, tk=tk, n_chunks=n_chunks)
    return pl.pallas_call(
        body,
        grid=(nb, hq, seq // tq),
        in_specs=[pl.BlockSpec((None, tq, LANES), lambda b, h, i: (b, i, h)),
                  pl.BlockSpec((None, seq, LANES), lambda b, h, i: (b, 0, h // rep)),
                  pl.BlockSpec((None, seq, LANES), lambda b, h, i: (b, 0, h // rep))],
        out_specs=pl.BlockSpec((None, tq, LANES), lambda b, h, i: (b, i, h)),
        out_shape=jax.ShapeDtypeStruct((nb, seq, wq), BF16),
        scratch_shapes=[pltpu.VMEM((LANES, tq), BF16), pltpu.VMEM((LANES, tq), BF16),
                        pltpu.VMEM((seq // kh, LANES, kh), BF16), pltpu.VMEM((1, tq), F32),
                        pltpu.VMEM((LANES, tq), F32), pltpu.VMEM((1, tq), F32),
                        pltpu.VMEM((tk, tq), F32), pltpu.VMEM((tk, tq), F32),
                        pltpu.VMEM((8, tq), F32), pltpu.VMEM((8, tq), F32),
                        pltpu.VMEM((tk, tq), BF16), pltpu.VMEM((tk, tq), BF16),
                        pltpu.VMEM((1, tq), F32), pltpu.VMEM((1, tq), F32)],
        compiler_params=_params("arbitrary", "arbitrary", "arbitrary"),
        name=name,
    )(q, k, v)


B_TU = 128
B_WIN = B_TU + 2 * B_SIDE


def _dilated_body(q_ref, kp_ref, kc_ref, kn_ref, vp_ref, vc_ref, vn_ref, bias_ref, o_ref, lse_ref, *, n_u):
    u0 = pl.program_id(2) * B_TU
    kw = jnp.concatenate([kp_ref[B_TU - B_SIDE:, :], kc_ref[...], kn_ref[:B_SIDE, :]], axis=0)
    vw = jnp.concatenate([vp_ref[B_TU - B_SIDE:, :], vc_ref[...], vn_ref[:B_SIDE, :]], axis=0)
    key_u = u0 - B_SIDE + lax.broadcasted_iota(jnp.int32, (2 * B_TU, B_WIN), 1)
    valid = (key_u >= 0) & (key_u < n_u)
    lane = lax.broadcasted_iota(jnp.int32, (B_TU, LANES), 1)
    low = lane < B_HD
    n_pairs = B_HEADS // 2
    lanes = [slice(hp * LANES, (hp + 1) * LANES) for hp in range(n_pairs)]
    scores = []
    for hp in range(n_pairs):
        qp = q_ref[:, lanes[hp]]
        zero = jnp.zeros_like(qp)
        q2 = jnp.concatenate([jnp.where(low, qp, zero), jnp.where(low, zero, qp)], axis=0)
        s = lax.dot_general(q2, kw[:, lanes[hp]], (((1,), (1,)), ((), ())), preferred_element_type=F32)
        scores.append(jnp.where(valid, s + bias_ref[hp], NEG_BIG))
    maxes = [jnp.max(s, axis=-1, keepdims=True) for s in scores]
    probs = [jnp.exp2(s - m) for s, m in zip(scores, maxes)]
    sums = [jnp.sum(p, axis=-1, keepdims=True) for p in probs]
    outs = [jnp.dot(p.astype(BF16), vw[:, lanes[hp]], preferred_element_type=F32) * (1.0 / l)
            for hp, (p, l) in enumerate(zip(probs, sums))]
    for hp in range(n_pairs):
        lse = jnp.broadcast_to(maxes[hp] + jnp.log2(sums[hp]), (2 * B_TU, LANES))
        o_ref[:, lanes[hp]] = jnp.where(low, outs[hp][:B_TU], outs[hp][B_TU:]).astype(o_ref.dtype)
        lse_ref[:, lanes[hp]] = jnp.where(low, lse[:B_TU], lse[B_TU:])


def _dilated(zg, bias, *, name):
    nb, dilation, n_u, _ = zg.shape
    nblk = n_u // B_TU
    hw = B_HEADS * B_HD

    def spec(part, shift):
        def imap(b, r, i):
            return (b, r, jnp.clip(i + shift, 0, nblk - 1), part)
        return pl.BlockSpec((None, None, B_TU, hw), imap)

    out_spec = pl.BlockSpec((None, None, B_TU, hw), lambda b, r, i: (b, r, i, 0))
    return pl.pallas_call(
        functools.partial(_dilated_body, n_u=n_u),
        grid=(nb, dilation, nblk),
        in_specs=[spec(0, 0), spec(1, -1), spec(1, 0), spec(1, 1), spec(2, -1), spec(2, 0), spec(2, 1),
                  pl.BlockSpec(bias.shape, lambda b, r, i: (0, 0, 0))],
        out_specs=[out_spec, out_spec],
        out_shape=[jax.ShapeDtypeStruct((nb, dilation, n_u, hw), BF16),
                   jax.ShapeDtypeStruct((nb, dilation, n_u, hw), F32)],
        compiler_params=_params("arbitrary", "arbitrary", "arbitrary"),
        name=name,
    )(zg, zg, zg, zg, zg, zg, zg, bias)


def _merge_body(x_ref, oa_ref, oc_ref, ob0_ref, ob1_ref, ob2_ref, l0_ref, l1_ref, l2_ref, gmix_ref, wg_ref,
                wa_ref, wb_ref, wc_ref, wo_ref, out_ref, on_ref, ln_ref):
    tm = x_ref.shape[0]
    x = x_ref[...]
    gate = jnp.dot(_rms(x, gmix_ref[...]).astype(BF16), wg_ref[...], preferred_element_type=F32)
    n_lt = on_ref.shape[1]
    for g, (o_ref, l_ref) in enumerate(((ob0_ref, l0_ref), (ob1_ref, l1_ref), (ob2_ref, l2_ref))):
        d = B_PAIRS[g][1]
        for r in range(d):
            for c in range(n_lt):
                cols = slice(c * LANES, (c + 1) * LANES)
                on_ref[g, c, pl.ds(r, tm // d, stride=d), :] = o_ref[r, :, cols].astype(F32)
                ln_ref[g, c, pl.ds(r, tm // d, stride=d), :] = l_ref[r, :, cols]
    wide = lambda ref, g: jnp.concatenate([ref[g, c] for c in range(n_lt)], axis=1)
    l0, l1, l2 = wide(ln_ref, 0), wide(ln_ref, 1), wide(ln_ref, 2)
    mx = jnp.maximum(jnp.maximum(l0, l1), l2)
    e0, e1, e2 = jnp.exp2(l0 - mx), jnp.exp2(l1 - mx), jnp.exp2(l2 - mx)
    ob = e0 * wide(on_ref, 0) + e1 * wide(on_ref, 1) + e2 * wide(on_ref, 2)
    ob = (ob * (1.0 / (e0 + e1 + e2))).astype(BF16)
    sg = 1.0 / (1.0 + jnp.exp(-gate))
    mix = sg[:, 0:D_MODEL] * jnp.dot(oa_ref[...], wa_ref[...], preferred_element_type=F32)
    mix += sg[:, D_MODEL:2 * D_MODEL] * jnp.dot(ob, wb_ref[...], preferred_element_type=F32)
    mix += sg[:, 2 * D_MODEL:] * jnp.dot(oc_ref[...], wc_ref[...], preferred_element_type=F32)
    out_ref[...] = x + jnp.dot(mix.astype(BF16), wo_ref[...], preferred_element_type=F32)


def _merge(x, oa, oc, obs, lses, gmix, wg, wa, wb, wc, wo, *, seq, tm):
    rows = x.shape[0]
    tiles_per_seq = seq // tm
    row_spec = lambda w: pl.BlockSpec((tm, w), lambda i: (i, 0))
    full = _resident
    hw = B_HEADS * B_HD
    res_specs = [pl.BlockSpec((None, d, tm // d, hw), lambda i: (i // tiles_per_seq, 0, i % tiles_per_seq, 0))
                 for _, d in B_PAIRS]
    return pl.pallas_call(
        _merge_body,
        grid=(rows // tm,),
        in_specs=[row_spec(D_MODEL), row_spec(oa.shape[1]), row_spec(oc.shape[1])] + res_specs + res_specs
                 + [full(gmix), full(wg), full(wa), full(wb), full(wc), full(wo)],
        out_specs=row_spec(D_MODEL),
        out_shape=jax.ShapeDtypeStruct((rows, D_MODEL), F32),
        scratch_shapes=[pltpu.VMEM((B_GROUPS, hw // LANES, tm, LANES), F32),
                        pltpu.VMEM((B_GROUPS, hw // LANES, tm, LANES), F32)],
        compiler_params=_params("arbitrary"),
        name="merge_out",
    )(x, oa, oc, *obs, *lses, gmix, wg, wa, wb, wc, wo)


FFN_CHUNK = 1024


def _ffn_body(x_ref, g_ref, wup_ref, wdn_ref, gfin_ref, out_ref, *, final):
    x = x_ref[...]
    h = _rms(x, g_ref[...]).astype(BF16)
    acc = x
    for c in range(D_FF // FFN_CHUNK):
        cs = slice(c * FFN_CHUNK, (c + 1) * FFN_CHUNK)
        u = jnp.dot(h, wup_ref[:, cs], preferred_element_type=F32)
        a = jnp.square(jnp.maximum(u, 0.0)).astype(BF16)
        acc = acc + jnp.dot(a, wdn_ref[cs, :], preferred_element_type=F32)
    if final:
        acc = _rms(acc, gfin_ref[...])
    out_ref[...] = acc


def _ffn(x, g, wup, wdn, gfin, *, final, tm):
    rows = x.shape[0]
    full = _resident
    return pl.pallas_call(
        functools.partial(_ffn_body, final=final),
        grid=(rows // tm,),
        in_specs=[pl.BlockSpec((tm, D_MODEL), lambda i: (i, 0)), full(g), full(wup), full(wdn), full(gfin)],
        out_specs=pl.BlockSpec((tm, D_MODEL), lambda i: (i, 0)),
        out_shape=jax.ShapeDtypeStruct((rows, D_MODEL), F32),
        compiler_params=_params("arbitrary"),
        name="ffn_final" if final else "ffn",
    )(x, g, wup, wdn, gfin)


def _pad_heads(w, n_heads, hd, lane_off=0):
    k = w.shape[0]
    w = w.reshape(k, n_heads, hd)
    w = jnp.pad(w, ((0, 0), (0, 0), (lane_off, LANES - lane_off - hd)))
    return w.reshape(k, n_heads * LANES)


def _pad_head_rows(w, n_heads, hd):
    n = w.shape[1]
    return jnp.pad(w.reshape(n_heads, hd, n), ((0, 0), (0, LANES - hd), (0, 0))).reshape(n_heads * LANES, n)


def _rot_cols(w):
    half = w.shape[-1] // 2
    return jnp.concatenate([-w[..., half:], w[..., :half]], axis=-1)


def _rot_axial(w):
    half = C_HD // 2
    return jnp.concatenate([_rot_cols(w[..., :half]), _rot_cols(w[..., half:])], axis=-1)


def _swap_axial(g):
    q = C_HD // 4
    return jnp.concatenate([g[q:2 * q], g[:q], g[3 * q:], g[2 * q:3 * q]], axis=-1)


def _rope_angles(pos, half):
    freqs = ROPE_THETA ** (-jnp.arange(half, dtype=F32) / half)
    return pos.astype(F32)[:, None] * freqs[None, :]


def _tables(seq):
    pos = jnp.arange(seq, dtype=jnp.int32)
    ang = _rope_angles(pos, A_ROPE // 2)
    cos_r = jnp.tile(jnp.cos(ang), (1, 2))
    sin_r = jnp.tile(jnp.sin(ang), (1, 2))
    zeros = lambda w: jnp.zeros((seq, w), F32)
    cos_k = jnp.concatenate([zeros(A_NOPE), cos_r, zeros(LANES - A_NOPE - A_ROPE)], axis=1)
    sin_k = jnp.concatenate([zeros(A_NOPE), sin_r, zeros(LANES - A_NOPE - A_ROPE)], axis=1)
    scale_a = (A_NOPE + A_ROPE) ** -0.5 * LOG2_E
    nope = jnp.concatenate([jnp.ones((seq, A_NOPE), F32), zeros(LANES - A_NOPE)], axis=1)
    cos_q = (nope + cos_k) * scale_a
    sin_q = sin_k * scale_a
    quarter = C_HD // 4
    ang_r = _rope_angles(pos // GRID_W, quarter)
    ang_c = _rope_angles(pos % GRID_W, quarter)
    cos_c = jnp.concatenate([jnp.tile(jnp.cos(ang_r), (1, 2)), jnp.tile(jnp.cos(ang_c), (1, 2)), zeros(LANES - C_HD)], axis=1)
    sin_c = jnp.concatenate([jnp.tile(jnp.sin(ang_r), (1, 2)), jnp.tile(jnp.sin(ang_c), (1, 2)), zeros(LANES - C_HD)], axis=1)
    return jnp.stack([cos_q, sin_q, cos_k, sin_k, cos_c, sin_c], axis=0)


def _t5_bucket(rel):
    half = NUM_BUCKETS // 2
    max_exact = half // 2
    ret = jnp.where(rel > 0, half, 0)
    n = jnp.abs(rel)
    nf = jnp.maximum(n, 1).astype(F32)
    large = max_exact + (jnp.log(nf / max_exact) / math.log(MAX_DISTANCE / max_exact) * (half - max_exact)).astype(jnp.int32)
    large = jnp.minimum(large, half - 1)
    return ret + jnp.where(n < max_exact, n, large)


def _band_bias(t5_table, group, dilation):
    offs = dilation * jnp.arange(-B_SIDE, B_SIDE + 1, dtype=jnp.int32)
    bias_hj = t5_table[_t5_bucket(offs)][:, group * B_HEADS:(group + 1) * B_HEADS].T.astype(F32) * LOG2_E
    period = B_WIN + B_TU + 1
    neg = jnp.full((B_HEADS, B_TU), NEG_BIG, F32)
    padded = jnp.concatenate([neg, bias_hj, neg], axis=1)
    flow = jnp.tile(padded, (1, B_TU + 1))[:, :B_TU * (period - 1)].reshape(B_HEADS, B_TU, period - 1)
    tiles = flow[:, :, B_TU:]
    return tiles.reshape(B_HEADS // 2, 2 * B_TU, B_WIN)


def _layer_weights(l, norm_mix, w_in, a_q_norm, a_kv_norm, a_w_uq, a_w_ukv, c_q_norm, c_k_norm,
                   w_br_a, w_br_b, w_br_c, w_out, norm_ffn, w_up, w_down):
    w = w_in[l]
    o = 0
    cols = []
    for width in (A_Q_LORA, A_KV_LORA, A_ROPE, 3 * B_GROUPS * B_HEADS * B_HD, C_HEADS * C_HD,
                  C_KV_HEADS * C_HD, C_KV_HEADS * C_HD, N_BRANCH * D_MODEL):
        cols.append(w[:, o:o + width])
        o += width
    w_cq, w_ckv, w_kr, w_b, w_qc, w_kc, w_vc, w_gate = cols
    kr_p = _pad_heads(w_kr, 1, A_ROPE, A_NOPE)
    kr_rot_p = _pad_heads(_rot_cols(w_kr), 1, A_ROPE, A_NOPE)
    qc = w_qc.reshape(D_MODEL, C_HEADS, C_HD)
    kc = w_kc.reshape(D_MODEL, C_KV_HEADS, C_HD)
    w1 = jnp.concatenate([
        w_cq, w_ckv, kr_p, kr_rot_p,
        _pad_heads(w_qc, C_HEADS, C_HD), _pad_heads(_rot_axial(qc).reshape(D_MODEL, -1), C_HEADS, C_HD),
        _pad_heads(w_kc, C_KV_HEADS, C_HD), _pad_heads(_rot_axial(kc).reshape(D_MODEL, -1), C_KV_HEADS, C_HD),
        _pad_heads(w_vc, C_KV_HEADS, C_HD)], axis=1).astype(BF16)
    uq = a_w_uq[l].reshape(A_Q_LORA, A_HEADS, A_NOPE + A_ROPE)
    uq_rot = jnp.concatenate([jnp.zeros((A_Q_LORA, A_HEADS, A_NOPE), F32), _rot_cols(uq[..., A_NOPE:])], axis=-1)
    wq2 = jnp.concatenate([_pad_heads(uq.reshape(A_Q_LORA, -1), A_HEADS, A_NOPE + A_ROPE),
                           _pad_heads(uq_rot.reshape(A_Q_LORA, -1), A_HEADS, A_NOPE + A_ROPE)], axis=1).astype(BF16)
    ukv = a_w_ukv[l].reshape(A_KV_LORA, A_HEADS, A_NOPE + A_V)
    wkv2 = jnp.concatenate([_pad_heads(ukv[..., :A_NOPE].reshape(A_KV_LORA, -1), A_HEADS, A_NOPE),
                            _pad_heads(ukv[..., A_NOPE:].reshape(A_KV_LORA, -1), A_HEADS, A_V)], axis=1).astype(BF16)
    pad_gain = lambda g: jnp.pad(g, (0, LANES - C_HD))
    scale_c = C_HD ** -0.5 * LOG2_E
    cg = jnp.stack([pad_gain(c_q_norm[l]) * scale_c, pad_gain(_swap_axial(c_q_norm[l])) * scale_c,
                    pad_gain(c_k_norm[l]), pad_gain(_swap_axial(c_k_norm[l]))], axis=0)
    wb5 = w_b.reshape(D_MODEL, 3, B_GROUPS, B_HEADS * B_HD)
    wb5 = wb5 * jnp.array([B_HD ** -0.5 * LOG2_E, 1.0, 1.0], F32)[None, :, None, None]
    w_bq = jnp.transpose(wb5, (0, 2, 1, 3)).reshape(D_MODEL, -1).astype(BF16)
    return dict(
        gmix=norm_mix[l][None, :], w1=w1, gq=a_q_norm[l][None, :], gkv=a_kv_norm[l][None, :], wq2=wq2, wkv2=wkv2,
        cg=cg, w_bq=w_bq, w_gate=w_gate.astype(BF16),
        wa=_pad_head_rows(w_br_a[l], A_HEADS, A_V).astype(BF16), wb=w_br_b[l].astype(BF16),
        wc=_pad_head_rows(w_br_c[l], C_HEADS, C_HD).astype(BF16), wo=w_out[l].astype(BF16),
        gffn=norm_ffn[l][None, :], wup=w_up[l].astype(BF16), wdn=w_down[l].astype(BF16))


def _encoder_layer(x, wts, tabs, biases, *, nb, seq, final, final_norm, tm, tq, tk):
    qa, ka, va, qc, kc, vc = _prep_ac(x, wts["gmix"], wts["w1"], wts["gq"], wts["gkv"], wts["wq2"], wts["wkv2"],
                                      wts["cg"], tabs, seq=seq, tm=tm)
    zbs = _proj_b(x, wts["gmix"], wts["w_bq"], nb=nb, seq=seq, tm=tm)

    per_seq = lambda a: a.reshape(nb, seq, -1)
    oa = _flash(per_seq(qa), per_seq(ka), per_seq(va), rep=1, tq=tq, tk=tk, name="flash_a").reshape(nb * seq, -1)
    oc = _flash(per_seq(qc), per_seq(kc), per_seq(vc), rep=C_HEADS // C_KV_HEADS, tq=tq, tk=tk,
                name="flash_c").reshape(nb * seq, -1)

    obs, lses = [], []
    for g in range(B_GROUPS):
        o_g, lse_g = _dilated(zbs[g], biases[g], name=f"dilated_{g}")
        obs.append(o_g)
        lses.append(lse_g)

    x = _merge(x, oa, oc, obs, lses, wts["gmix"], wts["w_gate"], wts["wa"], wts["wb"], wts["wc"], wts["wo"],
               seq=seq, tm=tm)
    return _ffn(x, wts["gffn"], wts["wup"], wts["wdn"], final_norm, final=final, tm=tm)


def _trunk(x, norm_mix, w_in, a_q_norm, a_kv_norm, a_w_uq, a_w_ukv, c_q_norm, c_k_norm,
           w_br_a, w_br_b, w_br_c, w_out, norm_ffn, w_up, w_down, t5_table, final_norm, *, tm, tq, tk):
    nb, seq, _ = x.shape
    tabs = _tables(seq)
    biases = [_band_bias(t5_table, g, d) for g, (_, d) in enumerate(B_PAIRS)]
    depth = w_in.shape[0]
    xr = x.reshape(nb * seq, D_MODEL)
    for l in range(depth):
        wts = _layer_weights(l, norm_mix, w_in, a_q_norm, a_kv_norm, a_w_uq, a_w_ukv, c_q_norm, c_k_norm,
                             w_br_a, w_br_b, w_br_c, w_out, norm_ffn, w_up, w_down)
        xr = _encoder_layer(xr, wts, tabs, biases, nb=nb, seq=seq, final=(l == depth - 1),
                            final_norm=final_norm[None, :], tm=tm, tq=tq, tk=tk)
    return xr.reshape(nb, seq, D_MODEL)


def kernel(x_prompt, x_sample, norm_mix, w_in, a_q_norm, a_kv_norm, a_w_uq, a_w_ukv, c_q_norm, c_k_norm,
           w_br_a, w_br_b, w_br_c, w_out, norm_ffn, w_up, w_down, t5_table, final_norm):
    assert x_prompt.shape[1:] == x_sample.shape[1:]
    n_prompt = x_prompt.shape[0]
    x = jnp.concatenate([x_prompt, x_sample], axis=0)
    y = _trunk(x, norm_mix, w_in, a_q_norm, a_kv_norm, a_w_uq, a_w_ukv, c_q_norm, c_k_norm,
               w_br_a, w_br_b, w_br_c, w_out, norm_ffn, w_up, w_down, t5_table, final_norm,
               tm=512, tq=2048, tk=512)
    return (y[:n_prompt], y[n_prompt:])
```

```python
import functools
import math

import jax
import jax.numpy as jnp
from jax import lax
from jax.experimental import pallas as pl
from jax.experimental.pallas import tpu as pltpu

D_MODEL = 1024
GRID_W = 64
NORM_EPS = 1e-6
ROPE_THETA = 10000.0
NEG_BIG = -1e30
A_HEADS, A_NOPE, A_ROPE, A_V = 8, 64, 32, 64
A_Q_LORA, A_KV_LORA = 384, 256
B_PAIRS = ((128, 1), (512, 4), (2048, 16))
B_GROUPS, B_HEADS, B_HD = 3, 8, 64
C_HEADS, C_KV_HEADS, C_HD = 8, 2, 64
NUM_BUCKETS, MAX_DISTANCE = 32, 2048
D_FF = 4 * D_MODEL
N_BRANCH = 3
B_SIDE = 64

LANES = 128
VMEM_LIMIT = 48 * 1024 * 1024
FLASH_ROW_GROUP = 128
FLASH_SAFE_BITS = 64.0
FLASH_REF_OFFSET = 30.0

LOG2_E = math.log2(math.e)

BF16 = jnp.bfloat16
F32 = jnp.float32


def _params(*sem):
    return pltpu.CompilerParams(dimension_semantics=sem, vmem_limit_bytes=VMEM_LIMIT)


def _resident(a):
    return pl.BlockSpec(a.shape, lambda i: (0,) * a.ndim, pipeline_mode=pl.Buffered(1))


def _rms(x, g):
    return x * lax.rsqrt(jnp.mean(x * x, axis=-1, keepdims=True) + NORM_EPS) * g


_C_CQ = (0, 384)
_C_CKV = (384, 640)
_C_KR = (640, 768)
_C_KRR = (768, 896)
_C_QC = (896, 1920)
_C_QCR = (1920, 2944)
_C_KC = (2944, 3200)
_C_KCR = (3200, 3456)
_C_VC = (3456, 3712)


def _prep_ac_body(x_ref, gmix_ref, w1_ref, gq_ref, gkv_ref, wq2_ref, wkv2_ref, cg_ref, tab_ref,
                  qa_ref, ka_ref, va_ref, qc_ref, kc_ref, vc_ref):
    h = _rms(x_ref[...], gmix_ref[...]).astype(BF16)
    z = jnp.dot(h, w1_ref[...], preferred_element_type=F32)
    cqn = _rms(z[:, _C_CQ[0]:_C_CQ[1]], gq_ref[...]).astype(BF16)
    ckvn = _rms(z[:, _C_CKV[0]:_C_CKV[1]], gkv_ref[...]).astype(BF16)
    qq = jnp.dot(cqn, wq2_ref[...], preferred_element_type=F32)
    kv = jnp.dot(ckvn, wkv2_ref[...], preferred_element_type=F32)
    cos_q, sin_q, cos_k, sin_k, cos_c, sin_c = (tab_ref[t] for t in range(6))
    krope = z[:, _C_KR[0]:_C_KR[1]] * cos_k + z[:, _C_KRR[0]:_C_KRR[1]] * sin_k
    lane = lax.broadcasted_iota(jnp.int32, (1, LANES), 1)
    ones_col = (lane == A_V).astype(F32)
    ref_col = (lane == LANES - 1).astype(F32)
    hw = A_HEADS * LANES
    for hd in range(A_HEADS):
        sl = slice(hd * LANES, (hd + 1) * LANES)
        sr = slice(hw + hd * LANES, hw + (hd + 1) * LANES)
        qa_ref[:, sl] = (qq[:, sl] * cos_q + qq[:, sr] * sin_q).astype(BF16)
        ka_ref[:, sl] = (kv[:, sl] + krope + ref_col).astype(BF16)
        va_ref[:, sl] = (kv[:, sr] + ones_col).astype(BF16)
    gq_cos = cg_ref[0:1, :] * cos_c
    gq_sin = cg_ref[1:2, :] * sin_c
    gk_cos = cg_ref[2:3, :] * cos_c
    gk_sin = cg_ref[3:4, :] * sin_c
    for hd in range(C_HEADS):
        sl = slice(hd * LANES, (hd + 1) * LANES)
        q = z[:, _C_QC[0] + hd * LANES:_C_QC[0] + (hd + 1) * LANES]
        qr = z[:, _C_QCR[0] + hd * LANES:_C_QCR[0] + (hd + 1) * LANES]
        inv = lax.rsqrt(jnp.sum(q * q, axis=-1, keepdims=True) * (1.0 / C_HD) + NORM_EPS)
        qc_ref[:, sl] = ((q * gq_cos + qr * gq_sin) * inv).astype(BF16)
    for hd in range(C_KV_HEADS):
        sl = slice(hd * LANES, (hd + 1) * LANES)
        k = z[:, _C_KC[0] + hd * LANES:_C_KC[0] + (hd + 1) * LANES]
        kr = z[:, _C_KCR[0] + hd * LANES:_C_KCR[0] + (hd + 1) * LANES]
        inv = lax.rsqrt(jnp.sum(k * k, axis=-1, keepdims=True) * (1.0 / C_HD) + NORM_EPS)
        kc_ref[:, sl] = ((k * gk_cos + kr * gk_sin) * inv + ref_col).astype(BF16)
        vc_ref[:, sl] = (z[:, _C_VC[0] + hd * LANES:_C_VC[0] + (hd + 1) * LANES] + ones_col).astype(BF16)


def _prep_ac(x, gmix, w1, gq, gkv, wq2, wkv2, cg, tabs, *, seq, tm):
    rows = x.shape[0]
    tiles_per_seq = seq // tm
    full = _resident
    row_spec = lambda w: pl.BlockSpec((tm, w), lambda i: (i, 0))
    wa, wc, wk = A_HEADS * LANES, C_HEADS * LANES, C_KV_HEADS * LANES
    return pl.pallas_call(
        _prep_ac_body,
        grid=(rows // tm,),
        in_specs=[row_spec(D_MODEL), full(gmix), full(w1), full(gq), full(gkv), full(wq2), full(wkv2), full(cg),
                  pl.BlockSpec((6, tm, LANES), lambda i: (0, i % tiles_per_seq, 0))],
        out_specs=[row_spec(wa), row_spec(wa), row_spec(wa), row_spec(wc), row_spec(wk), row_spec(wk)],
        out_shape=[jax.ShapeDtypeStruct((rows, w), BF16) for w in (wa, wa, wa, wc, wk, wk)],
        compiler_params=_params("arbitrary"),
        name="prep_ac",
    )(x, gmix, w1, gq, gkv, wq2, wkv2, cg, tabs)


def _proj_b_body(x_ref, g_ref, w_ref, o0_ref, o1_ref, o2_ref, z_ref):
    h = _rms(x_ref[...], g_ref[...]).astype(BF16)
    z = jnp.dot(h, w_ref[...], preferred_element_type=F32)
    tm = x_ref.shape[0]
    n_lt = z_ref.shape[0]
    for c in range(n_lt):
        z_ref[c] = z[:, c * LANES:(c + 1) * LANES]
    lt_per_group = n_lt // B_GROUPS
    for g, o_ref in enumerate((o0_ref, o1_ref, o2_ref)):
        d = B_PAIRS[g][1]
        for r in range(d):
            for c in range(lt_per_group):
                piece = z_ref[g * lt_per_group + c, pl.ds(r, tm // d, stride=d), :]
                o_ref[r, :, c * LANES:(c + 1) * LANES] = piece.astype(o_ref.dtype)


def _proj_b(x, g, w, *, nb, seq, tm):
    rows = x.shape[0]
    tiles_per_seq = seq // tm
    gw = 3 * B_HEADS * B_HD
    out_specs, out_shapes = [], []
    for _, d in B_PAIRS:
        out_specs.append(pl.BlockSpec((None, d, tm // d, gw), lambda i: (i // tiles_per_seq, 0, i % tiles_per_seq, 0)))
        out_shapes.append(jax.ShapeDtypeStruct((nb, d, seq // d, gw), BF16))
    return pl.pallas_call(
        _proj_b_body,
        grid=(rows // tm,),
        in_specs=[pl.BlockSpec((tm, D_MODEL), lambda i: (i, 0)), _resident(g), _resident(w)],
        out_specs=out_specs,
        out_shape=out_shapes,
        scratch_shapes=[pltpu.VMEM((w.shape[1] // LANES, tm, LANES), F32)],
        compiler_params=_params("arbitrary"),
        name="proj_b",
    )(x, g, w)


def _flash_body(q_ref, k_ref, v_ref, o_ref, qT_ref, qx_ref, vT_ref, ksq_ref, acc_ref, m_ref, s0_ref, s1_ref,
                mc0_ref, mc1_ref, p0_ref, p1_ref, al0_ref, al1_ref, *, tk, n_chunks, rep):
    tq = q_ref.shape[0]
    kh = vT_ref.shape[2]
    assert tk == kh
    n_kh = vT_ref.shape[0]

    @pl.when((pl.program_id(2) == 0) & (pl.program_id(1) % rep == 0))
    def _():
        def tr(n, ksq):
            off = pl.multiple_of(n * kh, kh)
            vT_ref[n] = v_ref[pl.ds(off, kh), :].astype(F32).T.astype(BF16)
            kk = k_ref[pl.ds(off, kh), :].astype(F32)
            return jnp.maximum(ksq, jnp.sum(kk * kk, axis=1, keepdims=True))
        ksq = lax.fori_loop(0, n_kh, tr, jnp.zeros((kh, 1), F32))
        ksq_ref[...] = jnp.full(ksq_ref.shape, jnp.max(ksq), F32)

    qT = q_ref[...].astype(F32).T
    bound = jnp.sqrt(jnp.sum(qT * qT, axis=0, keepdims=True) * ksq_ref[...]) * 1.01 + 1.0
    safe = jnp.max(bound) <= FLASH_SAFE_BITS
    row = lax.broadcasted_iota(jnp.int32, (LANES, tq), 0)
    qT_ref[...] = qT.astype(BF16)
    qx_ref[...] = jnp.where(row == LANES - 1, FLASH_REF_OFFSET - bound, qT).astype(BF16)
    acc_ref[...] = jnp.zeros(acc_ref.shape, F32)

    def fast_scores(n, p_ref):
        off = pl.multiple_of(n * kh, kh)
        s = jnp.dot(k_ref[pl.ds(off, kh), :], qx_ref[...], preferred_element_type=F32)
        p_ref[...] = jnp.exp2(s).astype(BF16)

    def fast_values(n, p_ref):
        acc_ref[...] = acc_ref[...] + jnp.dot(vT_ref[n], p_ref[...], preferred_element_type=F32)

    @pl.when(safe)
    def _():
        fast_scores(0, p0_ref)

        def quad(n0, last):
            fast_scores(n0 + 1, p1_ref)
            fast_values(n0, p0_ref)
            fast_scores(n0 + 2, p0_ref)
            fast_values(n0 + 1, p1_ref)
            fast_scores(n0 + 3, p1_ref)
            fast_values(n0 + 2, p0_ref)
            if not last:
                fast_scores(n0 + 4, p0_ref)
            fast_values(n0 + 3, p1_ref)

        def body(j, carry):
            quad(4 * j, False)
            return carry

        lax.fori_loop(0, n_kh // 4 - 1, body, 0)
        quad(n_kh - 4, True)

    @pl.when(jnp.logical_not(safe))
    def _():
        _flash_running_max(k_ref, qT_ref, vT_ref, acc_ref, m_ref, s0_ref, s1_ref, mc0_ref, mc1_ref,
                           p0_ref, p1_ref, al0_ref, al1_ref, tk=tk, n_chunks=n_chunks, tq=tq)

    acc = acc_ref[...]
    o_ref[...] = (acc * (1.0 / acc[A_V:A_V + 1, :])).T.astype(o_ref.dtype)


def _flash_running_max(k_ref, qT_ref, vT_ref, acc_ref, m_ref, s0_ref, s1_ref, mc0_ref, mc1_ref,
                       p0_ref, p1_ref, al0_ref, al1_ref, *, tk, n_chunks, tq):
    rg = FLASH_ROW_GROUP
    n_groups = tk // rg
    m_ref[...] = jnp.full(m_ref.shape, NEG_BIG, F32)

    def stage_a(c, g, s_ref, mc_ref):
        off = pl.multiple_of(c * tk + g * rg, rg)
        sg = jnp.dot(k_ref[pl.ds(off, rg), :], qT_ref[...], preferred_element_type=F32)
        s_ref[g * rg:(g + 1) * rg, :] = sg
        part = jnp.max(sg.reshape(rg // 8, 8, tq), axis=0)
        mc_ref[...] = part if g == 0 else jnp.maximum(mc_ref[...], part)

    def stage_b_head(mc_ref, al_ref):
        m_prev = m_ref[...]
        m_new = jnp.maximum(m_prev, jnp.max(mc_ref[...], axis=0, keepdims=True))
        al_ref[...] = jnp.exp2(m_prev - m_new)
        m_ref[...] = m_new

    def stage_b(g, s_ref, p_ref):
        rows = slice(g * rg, (g + 1) * rg)
        p_ref[rows, :] = jnp.exp2(s_ref[rows, :] - m_ref[...]).astype(BF16)

    def stage_c(c, p_ref, al_ref):
        pv = jnp.dot(vT_ref[c], p_ref[...], preferred_element_type=F32)
        acc_ref[...] = acc_ref[...] * al_ref[...] + pv

    def step(c_a, c_c, s_in, mc_in, s_out, mc_out, p_in, al_in, p_out, al_out):
        stage_b_head(mc_in, al_out)
        stage_c(c_c, p_in, al_in)
        for g in range(n_groups):
            stage_a(c_a, g, s_out, mc_out)
            stage_b(g, s_in, p_out)

    for g in range(n_groups):
        stage_a(0, g, s0_ref, mc0_ref)
    stage_b_head(mc0_ref, al0_ref)
    for g in range(n_groups):
        stage_a(1, g, s1_ref, mc1_ref)
        stage_b(g, s0_ref, p0_ref)

    def pair(j, carry):
        c0 = 2 * j
        step(jnp.minimum(c0 + 2, n_chunks - 1), c0, s1_ref, mc1_ref, s0_ref, mc0_ref, p0_ref, al0_ref, p1_ref, al1_ref)
        step(jnp.minimum(c0 + 3, n_chunks - 1), c0 + 1, s0_ref, mc0_ref, s1_ref, mc1_ref, p1_ref, al1_ref, p0_ref, al0_ref)
        return carry

    lax.fori_loop(0, n_chunks // 2, pair, 0)


def _flash(q, k, v, *, rep, tq, tk, name):
    nb, seq, wq = q.shape
    hq = wq // LANES
    n_chunks = seq // tk
    kh = tk
    assert n_chunks % 4 == 0
    body = functools.partial(_flash_body, tk=tk, n_chunks=n_chunks, rep=rep)
    return pl.pallas_call(
        body,
        grid=(nb, hq, seq // tq),
        in_specs=[pl.BlockSpec((None, tq, LANES), lambda b, h, i: (b, i, h)),
                  pl.BlockSpec((None, seq, LANES), lambda b, h, i: (b, 0, h // rep)),
                  pl.BlockSpec((None, seq, LANES), lambda b, h, i: (b, 0, h // rep))],
        out_specs=pl.BlockSpec((None, tq, LANES), lambda b, h, i: (b, i, h)),
        out_shape=jax.ShapeDtypeStruct((nb, seq, wq), BF16),
        scratch_shapes=[pltpu.VMEM((LANES, tq), BF16), pltpu.VMEM((LANES, tq), BF16),
                        pltpu.VMEM((seq // kh, LANES, kh), BF16), pltpu.VMEM((1, tq), F32),
                        pltpu.VMEM((LANES, tq), F32), pltpu.VMEM((1, tq), F32),
                        pltpu.VMEM((tk, tq), F32), pltpu.VMEM((tk, tq), F32),
                        pltpu.VMEM((8, tq), F32), pltpu.VMEM((8, tq), F32),
                        pltpu.VMEM((tk, tq), BF16), pltpu.VMEM((tk, tq), BF16),
                        pltpu.VMEM((1, tq), F32), pltpu.VMEM((1, tq), F32)],
        compiler_params=_params("arbitrary", "arbitrary", "arbitrary"),
        name=name,
    )(q, k, v)


B_TU = 128
B_WIN = B_TU + 2 * B_SIDE
B_SUB = 4


def _dilated_body(q_ref, kp_ref, kc_ref, kn_ref, vp_ref, vc_ref, vn_ref, bias_ref, o_ref, lse_ref, *, n_u):
    u0 = pl.program_id(2) * (B_SUB * B_TU)
    kall = jnp.concatenate([kp_ref[B_TU - B_SIDE:, :], kc_ref[...], kn_ref[:B_SIDE, :]], axis=0)
    vall = jnp.concatenate([vp_ref[B_TU - B_SIDE:, :], vc_ref[...], vn_ref[:B_SIDE, :]], axis=0)
    col = lax.broadcasted_iota(jnp.int32, (2 * B_TU, B_WIN), 1)
    lane = lax.broadcasted_iota(jnp.int32, (B_TU, LANES), 1)
    low = lane < B_HD
    n_pairs = B_HEADS // 2
    lanes = [slice(hp * LANES, (hp + 1) * LANES) for hp in range(n_pairs)]
    units = [(t, hp) for t in range(B_SUB) for hp in range(n_pairs)]
    scores = []
    for t, hp in units:
        key_u = u0 + t * B_TU - B_SIDE + col
        valid = (key_u >= 0) & (key_u < n_u)
        qp = q_ref[t * B_TU:(t + 1) * B_TU, lanes[hp]]
        zero = jnp.zeros_like(qp)
        q2 = jnp.concatenate([jnp.where(low, qp, zero), jnp.where(low, zero, qp)], axis=0)
        kw = kall[t * B_TU:t * B_TU + B_WIN, lanes[hp]]
        s = lax.dot_general(q2, kw, (((1,), (1,)), ((), ())), preferred_element_type=F32)
        scores.append(jnp.where(valid, s + bias_ref[hp], NEG_BIG))
    maxes = [jnp.max(s, axis=-1, keepdims=True) for s in scores]
    probs = [jnp.exp2(s - m) for s, m in zip(scores, maxes)]
    sums = [jnp.sum(p, axis=-1, keepdims=True) for p in probs]
    outs = [jnp.dot(p.astype(BF16), vall[t * B_TU:t * B_TU + B_WIN, lanes[hp]], preferred_element_type=F32) * (1.0 / l)
            for (t, hp), p, l in zip(units, probs, sums)]
    for n, (t, hp) in enumerate(units):
        rows = slice(t * B_TU, (t + 1) * B_TU)
        lse = jnp.broadcast_to(maxes[n] + jnp.log2(sums[n]), (2 * B_TU, LANES))
        o_ref[rows, lanes[hp]] = jnp.where(low, outs[n][:B_TU], outs[n][B_TU:]).astype(o_ref.dtype)
        lse_ref[rows, lanes[hp]] = jnp.where(low, lse[:B_TU], lse[B_TU:])


def _dilated(zg, bias, *, name):
    nb, dilation, n_u, _ = zg.shape
    tile = B_SUB * B_TU
    nstep = n_u // tile
    nblk = n_u // B_TU
    hw = B_HEADS * B_HD

    def main(part):
        return pl.BlockSpec((None, None, tile, hw), lambda b, r, i: (b, r, i, part))

    def halo(part, before):
        def imap(b, r, i):
            blk = i * B_SUB - 1 if before else (i + 1) * B_SUB
            return (b, r, jnp.clip(blk, 0, nblk - 1), part)
        return pl.BlockSpec((None, None, B_TU, hw), imap)

    out_spec = pl.BlockSpec((None, None, tile, hw), lambda b, r, i: (b, r, i, 0))
    return pl.pallas_call(
        functools.partial(_dilated_body, n_u=n_u),
        grid=(nb, dilation, nstep),
        in_specs=[main(0), halo(1, True), main(1), halo(1, False), halo(2, True), main(2), halo(2, False),
                  pl.BlockSpec(bias.shape, lambda b, r, i: (0, 0, 0))],
        out_specs=[out_spec, out_spec],
        out_shape=[jax.ShapeDtypeStruct((nb, dilation, n_u, hw), BF16),
                   jax.ShapeDtypeStruct((nb, dilation, n_u, hw), F32)],
        compiler_params=_params("arbitrary", "arbitrary", "arbitrary"),
        name=name,
    )(zg, zg, zg, zg, zg, zg, zg, bias)


def _merge_body(x_ref, oa_ref, oc_ref, ob0_ref, ob1_ref, ob2_ref, l0_ref, l1_ref, l2_ref, gmix_ref, wg_ref,
                wa_ref, wb_ref, wc_ref, wo_ref, out_ref, on_ref, ln_ref):
    tm = x_ref.shape[0]
    x = x_ref[...]
    gate = jnp.dot(_rms(x, gmix_ref[...]).astype(BF16), wg_ref[...], preferred_element_type=F32)
    n_lt = on_ref.shape[1]
    for g, (o_ref, l_ref) in enumerate(((ob0_ref, l0_ref), (ob1_ref, l1_ref), (ob2_ref, l2_ref))):
        d = B_PAIRS[g][1]
        for r in range(d):
            for c in range(n_lt):
                cols = slice(c * LANES, (c + 1) * LANES)
                on_ref[g, c, pl.ds(r, tm // d, stride=d), :] = o_ref[r, :, cols].astype(F32)
                ln_ref[g, c, pl.ds(r, tm // d, stride=d), :] = l_ref[r, :, cols]
    wide = lambda ref, g: jnp.concatenate([ref[g, c] for c in range(n_lt)], axis=1)
    l0, l1, l2 = wide(ln_ref, 0), wide(ln_ref, 1), wide(ln_ref, 2)
    mx = jnp.maximum(jnp.maximum(l0, l1), l2)
    e0, e1, e2 = jnp.exp2(l0 - mx), jnp.exp2(l1 - mx), jnp.exp2(l2 - mx)
    ob = e0 * wide(on_ref, 0) + e1 * wide(on_ref, 1) + e2 * wide(on_ref, 2)
    ob = (ob * (1.0 / (e0 + e1 + e2))).astype(BF16)
    sg = 1.0 / (1.0 + jnp.exp(-gate))
    mix = sg[:, 0:D_MODEL] * jnp.dot(oa_ref[...], wa_ref[...], preferred_element_type=F32)
    mix += sg[:, D_MODEL:2 * D_MODEL] * jnp.dot(ob, wb_ref[...], preferred_element_type=F32)
    mix += sg[:, 2 * D_MODEL:] * jnp.dot(oc_ref[...], wc_ref[...], preferred_element_type=F32)
    out_ref[...] = x + jnp.dot(mix.astype(BF16), wo_ref[...], preferred_element_type=F32)


def _merge(x, oa, oc, obs, lses, gmix, wg, wa, wb, wc, wo, *, seq, tm):
    rows = x.shape[0]
    tiles_per_seq = seq // tm
    row_spec = lambda w: pl.BlockSpec((tm, w), lambda i: (i, 0))
    full = _resident
    hw = B_HEADS * B_HD
    res_specs = [pl.BlockSpec((None, d, tm // d, hw), lambda i: (i // tiles_per_seq, 0, i % tiles_per_seq, 0))
                 for _, d in B_PAIRS]
    return pl.pallas_call(
        _merge_body,
        grid=(rows // tm,),
        in_specs=[row_spec(D_MODEL), row_spec(oa.shape[1]), row_spec(oc.shape[1])] + res_specs + res_specs
                 + [full(gmix), full(wg), full(wa), full(wb), full(wc), full(wo)],
        out_specs=row_spec(D_MODEL),
        out_shape=jax.ShapeDtypeStruct((rows, D_MODEL), F32),
        scratch_shapes=[pltpu.VMEM((B_GROUPS, hw // LANES, tm, LANES), F32),
                        pltpu.VMEM((B_GROUPS, hw // LANES, tm, LANES), F32)],
        compiler_params=_params("arbitrary"),
        name="merge_out",
    )(x, oa, oc, *obs, *lses, gmix, wg, wa, wb, wc, wo)


FFN_CHUNK = 1024


def _ffn_body(x_ref, g_ref, wup_ref, wdn_ref, gfin_ref, out_ref, *, final):
    x = x_ref[...]
    h = _rms(x, g_ref[...]).astype(BF16)
    acc = x
    for c in range(D_FF // FFN_CHUNK):
        cs = slice(c * FFN_CHUNK, (c + 1) * FFN_CHUNK)
        u = jnp.dot(h, wup_ref[:, cs], preferred_element_type=F32)
        a = jnp.square(jnp.maximum(u, 0.0)).astype(BF16)
        acc = acc + jnp.dot(a, wdn_ref[cs, :], preferred_element_type=F32)
    if final:
        acc = _rms(acc, gfin_ref[...])
    out_ref[...] = acc


def _ffn(x, g, wup, wdn, gfin, *, final, tm):
    rows = x.shape[0]
    full = _resident
    return pl.pallas_call(
        functools.partial(_ffn_body, final=final),
        grid=(rows // tm,),
        in_specs=[pl.BlockSpec((tm, D_MODEL), lambda i: (i, 0)), full(g), full(wup), full(wdn), full(gfin)],
        out_specs=pl.BlockSpec((tm, D_MODEL), lambda i: (i, 0)),
        out_shape=jax.ShapeDtypeStruct((rows, D_MODEL), F32),
        compiler_params=_params("arbitrary"),
        name="ffn_final" if final else "ffn",
    )(x, g, wup, wdn, gfin)


def _pad_heads(w, n_heads, hd, lane_off=0):
    k = w.shape[0]
    w = w.reshape(k, n_heads, hd)
    w = jnp.pad(w, ((0, 0), (0, 0), (lane_off, LANES - lane_off - hd)))
    return w.reshape(k, n_heads * LANES)


def _pad_head_rows(w, n_heads, hd):
    n = w.shape[1]
    return jnp.pad(w.reshape(n_heads, hd, n), ((0, 0), (0, LANES - hd), (0, 0))).reshape(n_heads * LANES, n)


def _rot_cols(w):
    half = w.shape[-1] // 2
    return jnp.concatenate([-w[..., half:], w[..., :half]], axis=-1)


def _rot_axial(w):
    half = C_HD // 2
    return jnp.concatenate([_rot_cols(w[..., :half]), _rot_cols(w[..., half:])], axis=-1)


def _swap_axial(g):
    q = C_HD // 4
    return jnp.concatenate([g[q:2 * q], g[:q], g[3 * q:], g[2 * q:3 * q]], axis=-1)


def _rope_angles(pos, half):
    freqs = ROPE_THETA ** (-jnp.arange(half, dtype=F32) / half)
    return pos.astype(F32)[:, None] * freqs[None, :]


def _tables(seq):
    pos = jnp.arange(seq, dtype=jnp.int32)
    ang = _rope_angles(pos, A_ROPE // 2)
    cos_r = jnp.tile(jnp.cos(ang), (1, 2))
    sin_r = jnp.tile(jnp.sin(ang), (1, 2))
    zeros = lambda w: jnp.zeros((seq, w), F32)
    cos_k = jnp.concatenate([zeros(A_NOPE), cos_r, zeros(LANES - A_NOPE - A_ROPE)], axis=1)
    sin_k = jnp.concatenate([zeros(A_NOPE), sin_r, zeros(LANES - A_NOPE - A_ROPE)], axis=1)
    scale_a = (A_NOPE + A_ROPE) ** -0.5 * LOG2_E
    nope = jnp.concatenate([jnp.ones((seq, A_NOPE), F32), zeros(LANES - A_NOPE)], axis=1)
    cos_q = (nope + cos_k) * scale_a
    sin_q = sin_k * scale_a
    quarter = C_HD // 4
    ang_r = _rope_angles(pos // GRID_W, quarter)
    ang_c = _rope_angles(pos % GRID_W, quarter)
    cos_c = jnp.concatenate([jnp.tile(jnp.cos(ang_r), (1, 2)), jnp.tile(jnp.cos(ang_c), (1, 2)), zeros(LANES - C_HD)], axis=1)
    sin_c = jnp.concatenate([jnp.tile(jnp.sin(ang_r), (1, 2)), jnp.tile(jnp.sin(ang_c), (1, 2)), zeros(LANES - C_HD)], axis=1)
    return jnp.stack([cos_q, sin_q, cos_k, sin_k, cos_c, sin_c], axis=0)


def _t5_bucket(rel):
    half = NUM_BUCKETS // 2
    max_exact = half // 2
    ret = jnp.where(rel > 0, half, 0)
    n = jnp.abs(rel)
    nf = jnp.maximum(n, 1).astype(F32)
    large = max_exact + (jnp.log(nf / max_exact) / math.log(MAX_DISTANCE / max_exact) * (half - max_exact)).astype(jnp.int32)
    large = jnp.minimum(large, half - 1)
    return ret + jnp.where(n < max_exact, n, large)


def _band_bias(t5_table, group, dilation):
    offs = dilation * jnp.arange(-B_SIDE, B_SIDE + 1, dtype=jnp.int32)
    bias_hj = t5_table[_t5_bucket(offs)][:, group * B_HEADS:(group + 1) * B_HEADS].T.astype(F32) * LOG2_E
    period = B_WIN + B_TU + 1
    neg = jnp.full((B_HEADS, B_TU), NEG_BIG, F32)
    padded = jnp.concatenate([neg, bias_hj, neg], axis=1)
    flow = jnp.tile(padded, (1, B_TU + 1))[:, :B_TU * (period - 1)].reshape(B_HEADS, B_TU, period - 1)
    tiles = flow[:, :, B_TU:]
    return tiles.reshape(B_HEADS // 2, 2 * B_TU, B_WIN)


def _layer_weights(l, norm_mix, w_in, a_q_norm, a_kv_norm, a_w_uq, a_w_ukv, c_q_norm, c_k_norm,
                   w_br_a, w_br_b, w_br_c, w_out, norm_ffn, w_up, w_down):
    w = w_in[l]
    o = 0
    cols = []
    for width in (A_Q_LORA, A_KV_LORA, A_ROPE, 3 * B_GROUPS * B_HEADS * B_HD, C_HEADS * C_HD,
                  C_KV_HEADS * C_HD, C_KV_HEADS * C_HD, N_BRANCH * D_MODEL):
        cols.append(w[:, o:o + width])
        o += width
    w_cq, w_ckv, w_kr, w_b, w_qc, w_kc, w_vc, w_gate = cols
    kr_p = _pad_heads(w_kr, 1, A_ROPE, A_NOPE)
    kr_rot_p = _pad_heads(_rot_cols(w_kr), 1, A_ROPE, A_NOPE)
    qc = w_qc.reshape(D_MODEL, C_HEADS, C_HD)
    kc = w_kc.reshape(D_MODEL, C_KV_HEADS, C_HD)
    w1 = jnp.concatenate([
        w_cq, w_ckv, kr_p, kr_rot_p,
        _pad_heads(w_qc, C_HEADS, C_HD), _pad_heads(_rot_axial(qc).reshape(D_MODEL, -1), C_HEADS, C_HD),
        _pad_heads(w_kc, C_KV_HEADS, C_HD), _pad_heads(_rot_axial(kc).reshape(D_MODEL, -1), C_KV_HEADS, C_HD),
        _pad_heads(w_vc, C_KV_HEADS, C_HD)], axis=1).astype(BF16)
    uq = a_w_uq[l].reshape(A_Q_LORA, A_HEADS, A_NOPE + A_ROPE)
    uq_rot = jnp.concatenate([jnp.zeros((A_Q_LORA, A_HEADS, A_NOPE), F32), _rot_cols(uq[..., A_NOPE:])], axis=-1)
    wq2 = jnp.concatenate([_pad_heads(uq.reshape(A_Q_LORA, -1), A_HEADS, A_NOPE + A_ROPE),
                           _pad_heads(uq_rot.reshape(A_Q_LORA, -1), A_HEADS, A_NOPE + A_ROPE)], axis=1).astype(BF16)
    ukv = a_w_ukv[l].reshape(A_KV_LORA, A_HEADS, A_NOPE + A_V)
    wkv2 = jnp.concatenate([_pad_heads(ukv[..., :A_NOPE].reshape(A_KV_LORA, -1), A_HEADS, A_NOPE),
                            _pad_heads(ukv[..., A_NOPE:].reshape(A_KV_LORA, -1), A_HEADS, A_V)], axis=1).astype(BF16)
    pad_gain = lambda g: jnp.pad(g, (0, LANES - C_HD))
    scale_c = C_HD ** -0.5 * LOG2_E
    cg = jnp.stack([pad_gain(c_q_norm[l]) * scale_c, pad_gain(_swap_axial(c_q_norm[l])) * scale_c,
                    pad_gain(c_k_norm[l]), pad_gain(_swap_axial(c_k_norm[l]))], axis=0)
    wb5 = w_b.reshape(D_MODEL, 3, B_GROUPS, B_HEADS * B_HD)
    wb5 = wb5 * jnp.array([B_HD ** -0.5 * LOG2_E, 1.0, 1.0], F32)[None, :, None, None]
    w_bq = jnp.transpose(wb5, (0, 2, 1, 3)).reshape(D_MODEL, -1).astype(BF16)
    return dict(
        gmix=norm_mix[l][None, :], w1=w1, gq=a_q_norm[l][None, :], gkv=a_kv_norm[l][None, :], wq2=wq2, wkv2=wkv2,
        cg=cg, w_bq=w_bq, w_gate=w_gate.astype(BF16),
        wa=_pad_head_rows(w_br_a[l], A_HEADS, A_V).astype(BF16), wb=w_br_b[l].astype(BF16),
        wc=_pad_head_rows(w_br_c[l], C_HEADS, C_HD).astype(BF16), wo=w_out[l].astype(BF16),
        gffn=norm_ffn[l][None, :], wup=w_up[l].astype(BF16), wdn=w_down[l].astype(BF16))


def _encoder_layer(x, wts, tabs, biases, *, nb, seq, final, final_norm, tm, tq, tk):
    qa, ka, va, qc, kc, vc = _prep_ac(x, wts["gmix"], wts["w1"], wts["gq"], wts["gkv"], wts["wq2"], wts["wkv2"],
                                      wts["cg"], tabs, seq=seq, tm=tm)
    zbs = _proj_b(x, wts["gmix"], wts["w_bq"], nb=nb, seq=seq, tm=tm)

    per_seq = lambda a: a.reshape(nb, seq, -1)
    oa = _flash(per_seq(qa), per_seq(ka), per_seq(va), rep=1, tq=tq, tk=tk, name="flash_a").reshape(nb * seq, -1)
    oc = _flash(per_seq(qc), per_seq(kc), per_seq(vc), rep=C_HEADS // C_KV_HEADS, tq=tq, tk=tk,
                name="flash_c").reshape(nb * seq, -1)

    obs, lses = [], []
    for g in range(B_GROUPS):
        o_g, lse_g = _dilated(zbs[g], biases[g], name=f"dilated_{g}")
        obs.append(o_g)
        lses.append(lse_g)

    x = _merge(x, oa, oc, obs, lses, wts["gmix"], wts["w_gate"], wts["wa"], wts["wb"], wts["wc"], wts["wo"],
               seq=seq, tm=tm)
    return _ffn(x, wts["gffn"], wts["wup"], wts["wdn"], final_norm, final=final, tm=tm)


def _trunk(x, norm_mix, w_in, a_q_norm, a_kv_norm, a_w_uq, a_w_ukv, c_q_norm, c_k_norm,
           w_br_a, w_br_b, w_br_c, w_out, norm_ffn, w_up, w_down, t5_table, final_norm, *, tm, tq, tk):
    nb, seq, _ = x.shape
    tabs = _tables(seq)
    biases = [_band_bias(t5_table, g, d) for g, (_, d) in enumerate(B_PAIRS)]
    depth = w_in.shape[0]
    xr = x.reshape(nb * seq, D_MODEL)
    for l in range(depth):
        wts = _layer_weights(l, norm_mix, w_in, a_q_norm, a_kv_norm, a_w_uq, a_w_ukv, c_q_norm, c_k_norm,
                             w_br_a, w_br_b, w_br_c, w_out, norm_ffn, w_up, w_down)
        xr = _encoder_layer(xr, wts, tabs, biases, nb=nb, seq=seq, final=(l == depth - 1),
                            final_norm=final_norm[None, :], tm=tm, tq=tq, tk=tk)
    return xr.reshape(nb, seq, D_MODEL)


def kernel(x_prompt, x_sample, norm_mix, w_in, a_q_norm, a_kv_norm, a_w_uq, a_w_ukv, c_q_norm, c_k_norm,
           w_br_a, w_br_b, w_br_c, w_out, norm_ffn, w_up, w_down, t5_table, final_norm):
    assert x_prompt.shape[1:] == x_sample.shape[1:]
    n_prompt = x_prompt.shape[0]
    x = jnp.concatenate([x_prompt, x_sample], axis=0)
    y = _trunk(x, norm_mix, w_in, a_q_norm, a_kv_norm, a_w_uq, a_w_ukv, c_q_norm, c_k_norm,
               w_br_a, w_br_b, w_br_c, w_out, norm_ffn, w_up, w_down, t5_table, final_norm,
               tm=512, tq=2048, tk=512)
    return (y[:n_prompt], y[n_prompt:])
```

```python
import functools
import math

import jax
import jax.numpy as jnp
from jax import lax
from jax.experimental import pallas as pl
from jax.experimental.pallas import tpu as pltpu

D_MODEL = 1024
GRID_W = 64
NORM_EPS = 1e-6
ROPE_THETA = 10000.0
NEG_BIG = -1e30
A_HEADS, A_NOPE, A_ROPE, A_V = 8, 64, 32, 64
A_Q_LORA, A_KV_LORA = 384, 256
B_PAIRS = ((128, 1), (512, 4), (2048, 16))
B_GROUPS, B_HEADS, B_HD = 3, 8, 64
C_HEADS, C_KV_HEADS, C_HD = 8, 2, 64
NUM_BUCKETS, MAX_DISTANCE = 32, 2048
D_FF = 4 * D_MODEL
N_BRANCH = 3
B_SIDE = 64

LANES = 128
VMEM_LIMIT = 48 * 1024 * 1024
FLASH_ROW_GROUP = 128
FLASH_SAFE_BITS = 64.0
FLASH_REF_OFFSET = 30.0

LOG2_E = math.log2(math.e)

BF16 = jnp.bfloat16
F32 = jnp.float32


def _params(*sem):
    return pltpu.CompilerParams(dimension_semantics=sem, vmem_limit_bytes=VMEM_LIMIT)


def _resident(a):
    return pl.BlockSpec(a.shape, lambda i: (0,) * a.ndim, pipeline_mode=pl.Buffered(1))


def _rms(x, g):
    return x * lax.rsqrt(jnp.mean(x * x, axis=-1, keepdims=True) + NORM_EPS) * g


_C_CQ = (0, 384)
_C_CKV = (384, 640)
_C_KR = (640, 768)
_C_KRR = (768, 896)
_C_QC = (896, 1920)
_C_QCR = (1920, 2944)
_C_KC = (2944, 3200)
_C_KCR = (3200, 3456)
_C_VC = (3456, 3712)


def _prep_ac_body(x_ref, gmix_ref, w1_ref, gq_ref, gkv_ref, wq2_ref, wkv2_ref, cg_ref, tab_ref,
                  qa_ref, ka_ref, va_ref, qc_ref, kc_ref, vc_ref):
    h = _rms(x_ref[...], gmix_ref[...]).astype(BF16)
    z = jnp.dot(h, w1_ref[...], preferred_element_type=F32)
    cqn = _rms(z[:, _C_CQ[0]:_C_CQ[1]], gq_ref[...]).astype(BF16)
    ckvn = _rms(z[:, _C_CKV[0]:_C_CKV[1]], gkv_ref[...]).astype(BF16)
    qq = jnp.dot(cqn, wq2_ref[...], preferred_element_type=F32)
    kv = jnp.dot(ckvn, wkv2_ref[...], preferred_element_type=F32)
    cos_q, sin_q, cos_k, sin_k, cos_c, sin_c = (tab_ref[t] for t in range(6))
    krope = z[:, _C_KR[0]:_C_KR[1]] * cos_k + z[:, _C_KRR[0]:_C_KRR[1]] * sin_k
    lane = lax.broadcasted_iota(jnp.int32, (1, LANES), 1)
    ones_col = (lane == A_V).astype(F32)
    ref_col = (lane == LANES - 1).astype(F32)
    hw = A_HEADS * LANES
    for hd in range(A_HEADS):
        sl = slice(hd * LANES, (hd + 1) * LANES)
        sr = slice(hw + hd * LANES, hw + (hd + 1) * LANES)
        qa_ref[hd, 0] = (qq[:, sl] * cos_q + qq[:, sr] * sin_q).T.astype(BF16)
        ka_ref[hd] = (kv[:, sl] + krope + ref_col).astype(BF16)
        va_ref[hd, 0] = (kv[:, sr] + ones_col).T.astype(BF16)
    gq_cos = cg_ref[0:1, :] * cos_c
    gq_sin = cg_ref[1:2, :] * sin_c
    gk_cos = cg_ref[2:3, :] * cos_c
    gk_sin = cg_ref[3:4, :] * sin_c
    for hd in range(C_HEADS):
        sl = slice(hd * LANES, (hd + 1) * LANES)
        q = z[:, _C_QC[0] + hd * LANES:_C_QC[0] + (hd + 1) * LANES]
        qr = z[:, _C_QCR[0] + hd * LANES:_C_QCR[0] + (hd + 1) * LANES]
        inv = lax.rsqrt(jnp.sum(q * q, axis=-1, keepdims=True) * (1.0 / C_HD) + NORM_EPS)
        qc_ref[hd, 0] = ((q * gq_cos + qr * gq_sin) * inv).T.astype(BF16)
    for hd in range(C_KV_HEADS):
        sl = slice(hd * LANES, (hd + 1) * LANES)
        k = z[:, _C_KC[0] + hd * LANES:_C_KC[0] + (hd + 1) * LANES]
        kr = z[:, _C_KCR[0] + hd * LANES:_C_KCR[0] + (hd + 1) * LANES]
        inv = lax.rsqrt(jnp.sum(k * k, axis=-1, keepdims=True) * (1.0 / C_HD) + NORM_EPS)
        kc_ref[hd] = ((k * gk_cos + kr * gk_sin) * inv + ref_col).astype(BF16)
        vc_ref[hd, 0] = (z[:, _C_VC[0] + hd * LANES:_C_VC[0] + (hd + 1) * LANES] + ones_col).T.astype(BF16)


def _prep_ac(x, gmix, w1, gq, gkv, wq2, wkv2, cg, tabs, *, seq, tm):
    rows = x.shape[0]
    tiles_per_seq = seq // tm
    full = _resident
    row_spec = lambda w: pl.BlockSpec((tm, w), lambda i: (i, 0))
    nb = rows // seq
    heads = (A_HEADS, A_HEADS, A_HEADS, C_HEADS, C_KV_HEADS, C_KV_HEADS)
    transposed = (True, False, True, True, False, True)
    head_spec = lambda h: pl.BlockSpec((None, h, tm, LANES), lambda i: (i // tiles_per_seq, 0, i % tiles_per_seq, 0))
    slab_spec = lambda h: pl.BlockSpec((None, h, 1, LANES, tm), lambda i: (i // tiles_per_seq, 0, i % tiles_per_seq, 0, 0))
    return pl.pallas_call(
        _prep_ac_body,
        grid=(rows // tm,),
        in_specs=[row_spec(D_MODEL), full(gmix), full(w1), full(gq), full(gkv), full(wq2), full(wkv2), full(cg),
                  pl.BlockSpec((6, tm, LANES), lambda i: (0, i % tiles_per_seq, 0))],
        out_specs=[slab_spec(h) if t else head_spec(h) for h, t in zip(heads, transposed)],
        out_shape=[jax.ShapeDtypeStruct((nb, h, seq // tm, LANES, tm) if t else (nb, h, seq, LANES), BF16)
                   for h, t in zip(heads, transposed)],
        compiler_params=_params("arbitrary"),
        name="prep_ac",
    )(x, gmix, w1, gq, gkv, wq2, wkv2, cg, tabs)


def _proj_b_body(x_ref, g_ref, w_ref, o0_ref, o1_ref, o2_ref, z_ref):
    h = _rms(x_ref[...], g_ref[...]).astype(BF16)
    z = jnp.dot(h, w_ref[...], preferred_element_type=F32)
    tm = x_ref.shape[0]
    n_lt = z_ref.shape[0]
    for c in range(n_lt):
        z_ref[c] = z[:, c * LANES:(c + 1) * LANES]
    lt_per_group = n_lt // B_GROUPS
    for g, o_ref in enumerate((o0_ref, o1_ref, o2_ref)):
        d = B_PAIRS[g][1]
        for r in range(d):
            for c in range(lt_per_group):
                piece = z_ref[g * lt_per_group + c, pl.ds(r, tm // d, stride=d), :]
                o_ref[r, :, c * LANES:(c + 1) * LANES] = piece.astype(o_ref.dtype)


def _proj_b(x, g, w, *, nb, seq, tm):
    rows = x.shape[0]
    tiles_per_seq = seq // tm
    gw = 3 * B_HEADS * B_HD
    out_specs, out_shapes = [], []
    for _, d in B_PAIRS:
        out_specs.append(pl.BlockSpec((None, d, tm // d, gw), lambda i: (i // tiles_per_seq, 0, i % tiles_per_seq, 0)))
        out_shapes.append(jax.ShapeDtypeStruct((nb, d, seq // d, gw), BF16))
    return pl.pallas_call(
        _proj_b_body,
        grid=(rows // tm,),
        in_specs=[pl.BlockSpec((tm, D_MODEL), lambda i: (i, 0)), _resident(g), _resident(w)],
        out_specs=out_specs,
        out_shape=out_shapes,
        scratch_shapes=[pltpu.VMEM((w.shape[1] // LANES, tm, LANES), F32)],
        compiler_params=_params("arbitrary"),
        name="proj_b",
    )(x, g, w)


def _flash_body(q_ref, k_ref, vT_ref, o_ref, qT_ref, qx_ref, ksq_ref, acc_ref, m_ref, s0_ref, s1_ref,
                mc0_ref, mc1_ref, p0_ref, p1_ref, al0_ref, al1_ref, *, tk, n_chunks, rep):
    tq = o_ref.shape[0]
    kh = vT_ref.shape[2]
    assert tk == kh
    n_kh = vT_ref.shape[0]

    @pl.when((pl.program_id(2) == 0) & (pl.program_id(1) % rep == 0))
    def _():
        def norms(n, ksq):
            off = pl.multiple_of(n * kh, kh)
            kk = k_ref[pl.ds(off, kh), :].astype(F32)
            return jnp.maximum(ksq, jnp.sum(kk * kk, axis=1, keepdims=True))
        ksq = lax.fori_loop(0, n_kh, norms, jnp.zeros((kh, 1), F32))
        ksq_ref[...] = jnp.full(ksq_ref.shape, jnp.max(ksq), F32)

    qT = jnp.concatenate([q_ref[j] for j in range(q_ref.shape[0])], axis=1).astype(F32)
    bound = jnp.sqrt(jnp.sum(qT * qT, axis=0, keepdims=True) * ksq_ref[...]) * 1.01 + 1.0
    safe = jnp.max(bound) <= FLASH_SAFE_BITS
    row = lax.broadcasted_iota(jnp.int32, (LANES, tq), 0)
    qT_ref[...] = qT.astype(BF16)
    qx_ref[...] = jnp.where(row == LANES - 1, FLASH_REF_OFFSET - bound, qT).astype(BF16)
    acc_ref[...] = jnp.zeros(acc_ref.shape, F32)

    def fast_scores(n, p_ref):
        off = pl.multiple_of(n * kh, kh)
        s = jnp.dot(k_ref[pl.ds(off, kh), :], qx_ref[...], preferred_element_type=F32)
        p_ref[...] = jnp.exp2(s).astype(BF16)

    def fast_values(n, p_ref):
        acc_ref[...] = acc_ref[...] + jnp.dot(vT_ref[n], p_ref[...], preferred_element_type=F32)

    @pl.when(safe)
    def _():
        fast_scores(0, p0_ref)

        def quad(n0, last):
            fast_scores(n0 + 1, p1_ref)
            fast_values(n0, p0_ref)
            fast_scores(n0 + 2, p0_ref)
            fast_values(n0 + 1, p1_ref)
            fast_scores(n0 + 3, p1_ref)
            fast_values(n0 + 2, p0_ref)
            if not last:
                fast_scores(n0 + 4, p0_ref)
            fast_values(n0 + 3, p1_ref)

        def body(j, carry):
            quad(4 * j, False)
            return carry

        lax.fori_loop(0, n_kh // 4 - 1, body, 0)
        quad(n_kh - 4, True)

    @pl.when(jnp.logical_not(safe))
    def _():
        _flash_running_max(k_ref, qT_ref, vT_ref, acc_ref, m_ref, s0_ref, s1_ref, mc0_ref, mc1_ref,
                           p0_ref, p1_ref, al0_ref, al1_ref, tk=tk, n_chunks=n_chunks, tq=tq)

    acc = acc_ref[...]
    o_ref[...] = (acc * (1.0 / acc[A_V:A_V + 1, :])).T.astype(o_ref.dtype)


def _flash_running_max(k_ref, qT_ref, vT_ref, acc_ref, m_ref, s0_ref, s1_ref, mc0_ref, mc1_ref,
                       p0_ref, p1_ref, al0_ref, al1_ref, *, tk, n_chunks, tq):
    rg = FLASH_ROW_GROUP
    n_groups = tk // rg
    m_ref[...] = jnp.full(m_ref.shape, NEG_BIG, F32)

    def stage_a(c, g, s_ref, mc_ref):
        off = pl.multiple_of(c * tk + g * rg, rg)
        sg = jnp.dot(k_ref[pl.ds(off, rg), :], qT_ref[...], preferred_element_type=F32)
        s_ref[g * rg:(g + 1) * rg, :] = sg
        part = jnp.max(sg.reshape(rg // 8, 8, tq), axis=0)
        mc_ref[...] = part if g == 0 else jnp.maximum(mc_ref[...], part)

    def stage_b_head(mc_ref, al_ref):
        m_prev = m_ref[...]
        m_new = jnp.maximum(m_prev, jnp.max(mc_ref[...], axis=0, keepdims=True))
        al_ref[...] = jnp.exp2(m_prev - m_new)
        m_ref[...] = m_new

    def stage_b(g, s_ref, p_ref):
        rows = slice(g * rg, (g + 1) * rg)
        p_ref[rows, :] = jnp.exp2(s_ref[rows, :] - m_ref[...]).astype(BF16)

    def stage_c(c, p_ref, al_ref):
        pv = jnp.dot(vT_ref[c], p_ref[...], preferred_element_type=F32)
        acc_ref[...] = acc_ref[...] * al_ref[...] + pv

    def step(c_a, c_c, s_in, mc_in, s_out, mc_out, p_in, al_in, p_out, al_out):
        stage_b_head(mc_in, al_out)
        stage_c(c_c, p_in, al_in)
        for g in range(n_groups):
            stage_a(c_a, g, s_out, mc_out)
            stage_b(g, s_in, p_out)

    for g in range(n_groups):
        stage_a(0, g, s0_ref, mc0_ref)
    stage_b_head(mc0_ref, al0_ref)
    for g in range(n_groups):
        stage_a(1, g, s1_ref, mc1_ref)
        stage_b(g, s0_ref, p0_ref)

    def pair(j, carry):
        c0 = 2 * j
        step(jnp.minimum(c0 + 2, n_chunks - 1), c0, s1_ref, mc1_ref, s0_ref, mc0_ref, p0_ref, al0_ref, p1_ref, al1_ref)
        step(jnp.minimum(c0 + 3, n_chunks - 1), c0 + 1, s0_ref, mc0_ref, s1_ref, mc1_ref, p1_ref, al1_ref, p0_ref, al0_ref)
        return carry

    lax.fori_loop(0, n_chunks // 2, pair, 0)


def _flash(q, k, vT, *, rep, tq, tk, name):
    nb, hq = q.shape[:2]
    seq = k.shape[2]
    assert vT.shape[2:] == (seq // tk, LANES, tk) and q.shape[2:] == (seq // tk, LANES, tk)
    n_chunks = seq // tk
    kh = tk
    assert n_chunks % 4 == 0
    body = functools.partial(_flash_body, tk=tk, n_chunks=n_chunks, rep=rep)
    return pl.pallas_call(
        body,
        grid=(nb, hq, seq // tq),
        in_specs=[pl.BlockSpec((None, None, tq // tk, LANES, tk), lambda b, h, i: (b, h, i, 0, 0)),
                  pl.BlockSpec((None, None, seq, LANES), lambda b, h, i: (b, h // rep, 0, 0)),
                  pl.BlockSpec((None, None, seq // kh, LANES, kh), lambda b, h, i: (b, h // rep, 0, 0, 0))],
        out_specs=pl.BlockSpec((None, None, tq, LANES), lambda b, h, i: (b, h, i, 0)),
        out_shape=jax.ShapeDtypeStruct((nb, hq, seq, LANES), BF16),
        scratch_shapes=[pltpu.VMEM((LANES, tq), BF16), pltpu.VMEM((LANES, tq), BF16),
                        pltpu.VMEM((1, tq), F32),
                        pltpu.VMEM((LANES, tq), F32), pltpu.VMEM((1, tq), F32),
                        pltpu.VMEM((tk, tq), F32), pltpu.VMEM((tk, tq), F32),
                        pltpu.VMEM((8, tq), F32), pltpu.VMEM((8, tq), F32),
                        pltpu.VMEM((tk, tq), BF16), pltpu.VMEM((tk, tq), BF16),
                        pltpu.VMEM((1, tq), F32), pltpu.VMEM((1, tq), F32)],
        compiler_params=_params("arbitrary", "arbitrary", "arbitrary"),
        name=name,
    )(q, k, vT)


B_TU = 128
B_WIN = B_TU + 2 * B_SIDE
B_SUB = 4


def _dilated_body(q_ref, kp_ref, kc_ref, kn_ref, vp_ref, vc_ref, vn_ref, bias_ref, o_ref, lse_ref, *, n_u):
    u0 = pl.program_id(2) * (B_SUB * B_TU)
    kall = jnp.concatenate([kp_ref[B_TU - B_SIDE:, :], kc_ref[...], kn_ref[:B_SIDE, :]], axis=0)
    vall = jnp.concatenate([vp_ref[B_TU - B_SIDE:, :], vc_ref[...], vn_ref[:B_SIDE, :]], axis=0)
    col = lax.broadcasted_iota(jnp.int32, (2 * B_TU, B_WIN), 1)
    lane = lax.broadcasted_iota(jnp.int32, (B_TU, LANES), 1)
    low = lane < B_HD
    n_pairs = B_HEADS // 2
    lanes = [slice(hp * LANES, (hp + 1) * LANES) for hp in range(n_pairs)]
    units = [(t, hp) for t in range(B_SUB) for hp in range(n_pairs)]
    scores = []
    for t, hp in units:
        key_u = u0 + t * B_TU - B_SIDE + col
        valid = (key_u >= 0) & (key_u < n_u)
        qp = q_ref[t * B_TU:(t + 1) * B_TU, lanes[hp]]
        zero = jnp.zeros_like(qp)
        q2 = jnp.concatenate([jnp.where(low, qp, zero), jnp.where(low, zero, qp)], axis=0)
        kw = kall[t * B_TU:t * B_TU + B_WIN, lanes[hp]]
        s = lax.dot_general(q2, kw, (((1,), (1,)), ((), ())), preferred_element_type=F32)
        scores.append(jnp.where(valid, s + bias_ref[hp], NEG_BIG))
    maxes = [jnp.max(s, axis=-1, keepdims=True) for s in scores]
    probs = [jnp.exp2(s - m) for s, m in zip(scores, maxes)]
    sums = [jnp.sum(p, axis=-1, keepdims=True) for p in probs]
    outs = [jnp.dot(p.astype(BF16), vall[t * B_TU:t * B_TU + B_WIN, lanes[hp]], preferred_element_type=F32) * (1.0 / l)
            for (t, hp), p, l in zip(units, probs, sums)]
    for n, (t, hp) in enumerate(units):
        rows = slice(t * B_TU, (t + 1) * B_TU)
        lse = jnp.broadcast_to(maxes[n] + jnp.log2(sums[n]), (2 * B_TU, LANES))
        o_ref[rows, lanes[hp]] = jnp.where(low, outs[n][:B_TU], outs[n][B_TU:]).astype(o_ref.dtype)
        lse_ref[rows, lanes[hp]] = jnp.where(low, lse[:B_TU], lse[B_TU:])


def _dilated(zg, bias, *, name):
    nb, dilation, n_u, _ = zg.shape
    tile = B_SUB * B_TU
    nstep = n_u // tile
    nblk = n_u // B_TU
    hw = B_HEADS * B_HD

    def main(part):
        return pl.BlockSpec((None, None, tile, hw), lambda b, r, i: (b, r, i, part))

    def halo(part, before):
        def imap(b, r, i):
            blk = i * B_SUB - 1 if before else (i + 1) * B_SUB
            return (b, r, jnp.clip(blk, 0, nblk - 1), part)
        return pl.BlockSpec((None, None, B_TU, hw), imap)

    out_spec = pl.BlockSpec((None, None, tile, hw), lambda b, r, i: (b, r, i, 0))
    return pl.pallas_call(
        functools.partial(_dilated_body, n_u=n_u),
        grid=(nb, dilation, nstep),
        in_specs=[main(0), halo(1, True), main(1), halo(1, False), halo(2, True), main(2), halo(2, False),
                  pl.BlockSpec(bias.shape, lambda b, r, i: (0, 0, 0))],
        out_specs=[out_spec, out_spec],
        out_shape=[jax.ShapeDtypeStruct((nb, dilation, n_u, hw), BF16),
                   jax.ShapeDtypeStruct((nb, dilation, n_u, hw), F32)],
        compiler_params=_params("arbitrary", "arbitrary", "arbitrary"),
        name=name,
    )(zg, zg, zg, zg, zg, zg, zg, bias)


def _merge_body(x_ref, oa_ref, oc_ref, ob0_ref, ob1_ref, ob2_ref, l0_ref, l1_ref, l2_ref, gmix_ref, wg_ref,
                wa_ref, wb_ref, wc_ref, wo_ref, out_ref, on_ref, ln_ref):
    tm = x_ref.shape[0]
    x = x_ref[...]
    gate = jnp.dot(_rms(x, gmix_ref[...]).astype(BF16), wg_ref[...], preferred_element_type=F32)
    n_lt = on_ref.shape[1]
    for g, (o_ref, l_ref) in enumerate(((ob0_ref, l0_ref), (ob1_ref, l1_ref), (ob2_ref, l2_ref))):
        d = B_PAIRS[g][1]
        for r in range(d):
            for c in range(n_lt):
                cols = slice(c * LANES, (c + 1) * LANES)
                on_ref[g, c, pl.ds(r, tm // d, stride=d), :] = o_ref[r, :, cols].astype(F32)
                ln_ref[g, c, pl.ds(r, tm // d, stride=d), :] = l_ref[r, :, cols]
    wide = lambda ref, g: jnp.concatenate([ref[g, c] for c in range(n_lt)], axis=1)
    l0, l1, l2 = wide(ln_ref, 0), wide(ln_ref, 1), wide(ln_ref, 2)
    mx = jnp.maximum(jnp.maximum(l0, l1), l2)
    e0, e1, e2 = jnp.exp2(l0 - mx), jnp.exp2(l1 - mx), jnp.exp2(l2 - mx)
    ob = e0 * wide(on_ref, 0) + e1 * wide(on_ref, 1) + e2 * wide(on_ref, 2)
    ob = (ob * (1.0 / (e0 + e1 + e2))).astype(BF16)
    sg = 1.0 / (1.0 + jnp.exp(-gate))
    heads_to_lanes = lambda ref: jnp.concatenate([ref[h] for h in range(ref.shape[0])], axis=1)
    mix = sg[:, 0:D_MODEL] * jnp.dot(heads_to_lanes(oa_ref), wa_ref[...], preferred_element_type=F32)
    mix += sg[:, D_MODEL:2 * D_MODEL] * jnp.dot(ob, wb_ref[...], preferred_element_type=F32)
    mix += sg[:, 2 * D_MODEL:] * jnp.dot(heads_to_lanes(oc_ref), wc_ref[...], preferred_element_type=F32)
    out_ref[...] = x + jnp.dot(mix.astype(BF16), wo_ref[...], preferred_element_type=F32)


def _merge(x, oa, oc, obs, lses, gmix, wg, wa, wb, wc, wo, *, seq, tm):
    rows = x.shape[0]
    tiles_per_seq = seq // tm
    row_spec = lambda w: pl.BlockSpec((tm, w), lambda i: (i, 0))
    full = _resident
    hw = B_HEADS * B_HD
    res_specs = [pl.BlockSpec((None, d, tm // d, hw), lambda i: (i // tiles_per_seq, 0, i % tiles_per_seq, 0))
                 for _, d in B_PAIRS]
    head_spec = lambda o: pl.BlockSpec((None, o.shape[1], tm, LANES),
                                       lambda i: (i // tiles_per_seq, 0, i % tiles_per_seq, 0))
    return pl.pallas_call(
        _merge_body,
        grid=(rows // tm,),
        in_specs=[row_spec(D_MODEL), head_spec(oa), head_spec(oc)] + res_specs + res_specs
                 + [full(gmix), full(wg), full(wa), full(wb), full(wc), full(wo)],
        out_specs=row_spec(D_MODEL),
        out_shape=jax.ShapeDtypeStruct((rows, D_MODEL), F32),
        scratch_shapes=[pltpu.VMEM((B_GROUPS, hw // LANES, tm, LANES), F32),
                        pltpu.VMEM((B_GROUPS, hw // LANES, tm, LANES), F32)],
        compiler_params=_params("arbitrary"),
        name="merge_out",
    )(x, oa, oc, *obs, *lses, gmix, wg, wa, wb, wc, wo)


FFN_CHUNK = 1024


def _ffn_body(x_ref, g_ref, wup_ref, wdn_ref, gfin_ref, out_ref, *, final):
    x = x_ref[...]
    h = _rms(x, g_ref[...]).astype(BF16)
    acc = x
    for c in range(D_FF // FFN_CHUNK):
        cs = slice(c * FFN_CHUNK, (c + 1) * FFN_CHUNK)
        u = jnp.dot(h, wup_ref[:, cs], preferred_element_type=F32)
        a = jnp.square(jnp.maximum(u, 0.0)).astype(BF16)
        acc = acc + jnp.dot(a, wdn_ref[cs, :], preferred_element_type=F32)
    if final:
        acc = _rms(acc, gfin_ref[...])
    out_ref[...] = acc


def _ffn(x, g, wup, wdn, gfin, *, final, tm):
    rows = x.shape[0]
    full = _resident
    return pl.pallas_call(
        functools.partial(_ffn_body, final=final),
        grid=(rows // tm,),
        in_specs=[pl.BlockSpec((tm, D_MODEL), lambda i: (i, 0)), full(g), full(wup), full(wdn), full(gfin)],
        out_specs=pl.BlockSpec((tm, D_MODEL), lambda i: (i, 0)),
        out_shape=jax.ShapeDtypeStruct((rows, D_MODEL), F32),
        compiler_params=_params("arbitrary"),
        name="ffn_final" if final else "ffn",
    )(x, g, wup, wdn, gfin)


def _pad_heads(w, n_heads, hd, lane_off=0):
    k = w.shape[0]
    w = w.reshape(k, n_heads, hd)
    w = jnp.pad(w, ((0, 0), (0, 0), (lane_off, LANES - lane_off - hd)))
    return w.reshape(k, n_heads * LANES)


def _pad_head_rows(w, n_heads, hd):
    n = w.shape[1]
    return jnp.pad(w.reshape(n_heads, hd, n), ((0, 0), (0, LANES - hd), (0, 0))).reshape(n_heads * LANES, n)


def _rot_cols(w):
    half = w.shape[-1] // 2
    return jnp.concatenate([-w[..., half:], w[..., :half]], axis=-1)


def _rot_axial(w):
    half = C_HD // 2
    return jnp.concatenate([_rot_cols(w[..., :half]), _rot_cols(w[..., half:])], axis=-1)


def _swap_axial(g):
    q = C_HD // 4
    return jnp.concatenate([g[q:2 * q], g[:q], g[3 * q:], g[2 * q:3 * q]], axis=-1)


def _rope_angles(pos, half):
    freqs = ROPE_THETA ** (-jnp.arange(half, dtype=F32) / half)
    return pos.astype(F32)[:, None] * freqs[None, :]


def _tables(seq):
    pos = jnp.arange(seq, dtype=jnp.int32)
    ang = _rope_angles(pos, A_ROPE // 2)
    cos_r = jnp.tile(jnp.cos(ang), (1, 2))
    sin_r = jnp.tile(jnp.sin(ang), (1, 2))
    zeros = lambda w: jnp.zeros((seq, w), F32)
    cos_k = jnp.concatenate([zeros(A_NOPE), cos_r, zeros(LANES - A_NOPE - A_ROPE)], axis=1)
    sin_k = jnp.concatenate([zeros(A_NOPE), sin_r, zeros(LANES - A_NOPE - A_ROPE)], axis=1)
    scale_a = (A_NOPE + A_ROPE) ** -0.5 * LOG2_E
    nope = jnp.concatenate([jnp.ones((seq, A_NOPE), F32), zeros(LANES - A_NOPE)], axis=1)
    cos_q = (nope + cos_k) * scale_a
    sin_q = sin_k * scale_a
    quarter = C_HD // 4
    ang_r = _rope_angles(pos // GRID_W, quarter)
    ang_c = _rope_angles(pos % GRID_W, quarter)
    cos_c = jnp.concatenate([jnp.tile(jnp.cos(ang_r), (1, 2)), jnp.tile(jnp.cos(ang_c), (1, 2)), zeros(LANES - C_HD)], axis=1)
    sin_c = jnp.concatenate([jnp.tile(jnp.sin(ang_r), (1, 2)), jnp.tile(jnp.sin(ang_c), (1, 2)), zeros(LANES - C_HD)], axis=1)
    return jnp.stack([cos_q, sin_q, cos_k, sin_k, cos_c, sin_c], axis=0)


def _t5_bucket(rel):
    half = NUM_BUCKETS // 2
    max_exact = half // 2
    ret = jnp.where(rel > 0, half, 0)
    n = jnp.abs(rel)
    nf = jnp.maximum(n, 1).astype(F32)
    large = max_exact + (jnp.log(nf / max_exact) / math.log(MAX_DISTANCE / max_exact) * (half - max_exact)).astype(jnp.int32)
    large = jnp.minimum(large, half - 1)
    return ret + jnp.where(n < max_exact, n, large)


def _band_bias(t5_table, group, dilation):
    offs = dilation * jnp.arange(-B_SIDE, B_SIDE + 1, dtype=jnp.int32)
    bias_hj = t5_table[_t5_bucket(offs)][:, group * B_HEADS:(group + 1) * B_HEADS].T.astype(F32) * LOG2_E
    period = B_WIN + B_TU + 1
    neg = jnp.full((B_HEADS, B_TU), NEG_BIG, F32)
    padded = jnp.concatenate([neg, bias_hj, neg], axis=1)
    flow = jnp.tile(padded, (1, B_TU + 1))[:, :B_TU * (period - 1)].reshape(B_HEADS, B_TU, period - 1)
    tiles = flow[:, :, B_TU:]
    return tiles.reshape(B_HEADS // 2, 2 * B_TU, B_WIN)


def _layer_weights(l, norm_mix, w_in, a_q_norm, a_kv_norm, a_w_uq, a_w_ukv, c_q_norm, c_k_norm,
                   w_br_a, w_br_b, w_br_c, w_out, norm_ffn, w_up, w_down):
    w = w_in[l]
    o = 0
    cols = []
    for width in (A_Q_LORA, A_KV_LORA, A_ROPE, 3 * B_GROUPS * B_HEADS * B_HD, C_HEADS * C_HD,
                  C_KV_HEADS * C_HD, C_KV_HEADS * C_HD, N_BRANCH * D_MODEL):
        cols.append(w[:, o:o + width])
        o += width
    w_cq, w_ckv, w_kr, w_b, w_qc, w_kc, w_vc, w_gate = cols
    kr_p = _pad_heads(w_kr, 1, A_ROPE, A_NOPE)
    kr_rot_p = _pad_heads(_rot_cols(w_kr), 1, A_ROPE, A_NOPE)
    qc = w_qc.reshape(D_MODEL, C_HEADS, C_HD)
    kc = w_kc.reshape(D_MODEL, C_KV_HEADS, C_HD)
    w1 = jnp.concatenate([
        w_cq, w_ckv, kr_p, kr_rot_p,
        _pad_heads(w_qc, C_HEADS, C_HD), _pad_heads(_rot_axial(qc).reshape(D_MODEL, -1), C_HEADS, C_HD),
        _pad_heads(w_kc, C_KV_HEADS, C_HD), _pad_heads(_rot_axial(kc).reshape(D_MODEL, -1), C_KV_HEADS, C_HD),
        _pad_heads(w_vc, C_KV_HEADS, C_HD)], axis=1).astype(BF16)
    uq = a_w_uq[l].reshape(A_Q_LORA, A_HEADS, A_NOPE + A_ROPE)
    uq_rot = jnp.concatenate([jnp.zeros((A_Q_LORA, A_HEADS, A_NOPE), F32), _rot_cols(uq[..., A_NOPE:])], axis=-1)
    wq2 = jnp.concatenate([_pad_heads(uq.reshape(A_Q_LORA, -1), A_HEADS, A_NOPE + A_ROPE),
                           _pad_heads(uq_rot.reshape(A_Q_LORA, -1), A_HEADS, A_NOPE + A_ROPE)], axis=1).astype(BF16)
    ukv = a_w_ukv[l].reshape(A_KV_LORA, A_HEADS, A_NOPE + A_V)
    wkv2 = jnp.concatenate([_pad_heads(ukv[..., :A_NOPE].reshape(A_KV_LORA, -1), A_HEADS, A_NOPE),
                            _pad_heads(ukv[..., A_NOPE:].reshape(A_KV_LORA, -1), A_HEADS, A_V)], axis=1).astype(BF16)
    pad_gain = lambda g: jnp.pad(g, (0, LANES - C_HD))
    scale_c = C_HD ** -0.5 * LOG2_E
    cg = jnp.stack([pad_gain(c_q_norm[l]) * scale_c, pad_gain(_swap_axial(c_q_norm[l])) * scale_c,
                    pad_gain(c_k_norm[l]), pad_gain(_swap_axial(c_k_norm[l]))], axis=0)
    wb5 = w_b.reshape(D_MODEL, 3, B_GROUPS, B_HEADS * B_HD)
    wb5 = wb5 * jnp.array([B_HD ** -0.5 * LOG2_E, 1.0, 1.0], F32)[None, :, None, None]
    w_bq = jnp.transpose(wb5, (0, 2, 1, 3)).reshape(D_MODEL, -1).astype(BF16)
    return dict(
        gmix=norm_mix[l][None, :], w1=w1, gq=a_q_norm[l][None, :], gkv=a_kv_norm[l][None, :], wq2=wq2, wkv2=wkv2,
        cg=cg, w_bq=w_bq, w_gate=w_gate.astype(BF16),
        wa=_pad_head_rows(w_br_a[l], A_HEADS, A_V).astype(BF16), wb=w_br_b[l].astype(BF16),
        wc=_pad_head_rows(w_br_c[l], C_HEADS, C_HD).astype(BF16), wo=w_out[l].astype(BF16),
        gffn=norm_ffn[l][None, :], wup=w_up[l].astype(BF16), wdn=w_down[l].astype(BF16))


def _encoder_layer(x, wts, tabs, biases, *, nb, seq, final, final_norm, tm, tq, tk):
    qa, ka, va, qc, kc, vc = _prep_ac(x, wts["gmix"], wts["w1"], wts["gq"], wts["gkv"], wts["wq2"], wts["wkv2"],
                                      wts["cg"], tabs, seq=seq, tm=tm)
    zbs = _proj_b(x, wts["gmix"], wts["w_bq"], nb=nb, seq=seq, tm=tm)

    assert tm == tk
    oa = _flash(qa, ka, va, rep=1, tq=tq, tk=tk, name="flash_a")
    oc = _flash(qc, kc, vc, rep=C_HEADS // C_KV_HEADS, tq=tq, tk=tk, name="flash_c")

    obs, lses = [], []
    for g in range(B_GROUPS):
        o_g, lse_g = _dilated(zbs[g], biases[g], name=f"dilated_{g}")
        obs.append(o_g)
        lses.append(lse_g)

    x = _merge(x, oa, oc, obs, lses, wts["gmix"], wts["w_gate"], wts["wa"], wts["wb"], wts["wc"], wts["wo"],
               seq=seq, tm=tm)
    return _ffn(x, wts["gffn"], wts["wup"], wts["wdn"], final_norm, final=final, tm=tm)


def _trunk(x, norm_mix, w_in, a_q_norm, a_kv_norm, a_w_uq, a_w_ukv, c_q_norm, c_k_norm,
           w_br_a, w_br_b, w_br_c, w_out, norm_ffn, w_up, w_down, t5_table, final_norm, *, tm, tq, tk):
    nb, seq, _ = x.shape
    tabs = _tables(seq)
    biases = [_band_bias(t5_table, g, d) for g, (_, d) in enumerate(B_PAIRS)]
    depth = w_in.shape[0]
    xr = x.reshape(nb * seq, D_MODEL)
    for l in range(depth):
        wts = _layer_weights(l, norm_mix, w_in, a_q_norm, a_kv_norm, a_w_uq, a_w_ukv, c_q_norm, c_k_norm,
                             w_br_a, w_br_b, w_br_c, w_out, norm_ffn, w_up, w_down)
        xr = _encoder_layer(xr, wts, tabs, biases, nb=nb, seq=seq, final=(l == depth - 1),
                            final_norm=final_norm[None, :], tm=tm, tq=tq, tk=tk)
    return xr.reshape(nb, seq, D_MODEL)


def kernel(x_prompt, x_sample, norm_mix, w_in, a_q_norm, a_kv_norm, a_w_uq, a_w_ukv, c_q_norm, c_k_norm,
           w_br_a, w_br_b, w_br_c, w_out, norm_ffn, w_up, w_down, t5_table, final_norm):
    assert x_prompt.shape[1:] == x_sample.shape[1:]
    n_prompt = x_prompt.shape[0]
    x = jnp.concatenate([x_prompt, x_sample], axis=0)
    y = _trunk(x, norm_mix, w_in, a_q_norm, a_kv_norm, a_w_uq, a_w_ukv, c_q_norm, c_k_norm,
               w_br_a, w_br_b, w_br_c, w_out, norm_ffn, w_up, w_down, t5_table, final_norm,
               tm=512, tq=2048, tk=512)
    return (y[:n_prompt], y[n_prompt:])
```

```python
import functools
import math

import jax
import jax.numpy as jnp
from jax import lax
from jax.experimental import pallas as pl
from jax.experimental.pallas import tpu as pltpu

D_MODEL = 1024
GRID_W = 64
NORM_EPS = 1e-6
ROPE_THETA = 10000.0
NEG_BIG = -1e30
A_HEADS, A_NOPE, A_ROPE, A_V = 8, 64, 32, 64
A_Q_LORA, A_KV_LORA = 384, 256
B_PAIRS = ((128, 1), (512, 4), (2048, 16))
B_GROUPS, B_HEADS, B_HD = 3, 8, 64
C_HEADS, C_KV_HEADS, C_HD = 8, 2, 64
NUM_BUCKETS, MAX_DISTANCE = 32, 2048
D_FF = 4 * D_MODEL
N_BRANCH = 3
B_SIDE = 64

LANES = 128
VMEM_LIMIT = 48 * 1024 * 1024
V_ROWS = 80
FAST_SLABS = 2
FLASH_ROW_GROUP = 128
FLASH_SAFE_BITS = 64.0
FLASH_REF_OFFSET = 30.0

LOG2_E = math.log2(math.e)

BF16 = jnp.bfloat16
F32 = jnp.float32


def _params(*sem):
    return pltpu.CompilerParams(dimension_semantics=sem, vmem_limit_bytes=VMEM_LIMIT)


def _resident(a):
    return pl.BlockSpec(a.shape, lambda i: (0,) * a.ndim, pipeline_mode=pl.Buffered(1))


def _rms(x, g):
    return x * lax.rsqrt(jnp.mean(x * x, axis=-1, keepdims=True) + NORM_EPS) * g


_C_CQ = (0, 384)
_C_CKV = (384, 640)
_C_KR = (640, 768)
_C_KRR = (768, 896)
_C_QC = (896, 1920)
_C_QCR = (1920, 2944)
_C_KC = (2944, 3200)
_C_KCR = (3200, 3456)
_C_VC = (3456, 3712)


def _prep_ac_body(x_ref, gmix_ref, w1_ref, gq_ref, gkv_ref, wq2_ref, wkv2_ref, cg_ref, tab_ref,
                  qa_ref, ka_ref, va_ref, qc_ref, kc_ref, vc_ref):
    h = _rms(x_ref[...], gmix_ref[...]).astype(BF16)
    z = jnp.dot(h, w1_ref[...], preferred_element_type=F32)
    cqn = _rms(z[:, _C_CQ[0]:_C_CQ[1]], gq_ref[...]).astype(BF16)
    ckvn = _rms(z[:, _C_CKV[0]:_C_CKV[1]], gkv_ref[...]).astype(BF16)
    qq = jnp.dot(cqn, wq2_ref[...], preferred_element_type=F32)
    kv = jnp.dot(ckvn, wkv2_ref[...], preferred_element_type=F32)
    cos_q, sin_q, cos_k, sin_k, cos_c, sin_c = (tab_ref[t] for t in range(6))
    krope = z[:, _C_KR[0]:_C_KR[1]] * cos_k + z[:, _C_KRR[0]:_C_KRR[1]] * sin_k
    lane = lax.broadcasted_iota(jnp.int32, (1, LANES), 1)
    ones_col = (lane == A_V).astype(F32)
    ref_col = (lane == LANES - 1).astype(F32)
    hw = A_HEADS * LANES
    for hd in range(A_HEADS):
        sl = slice(hd * LANES, (hd + 1) * LANES)
        sr = slice(hw + hd * LANES, hw + (hd + 1) * LANES)
        qa_ref[hd, 0] = (qq[:, sl] * cos_q + qq[:, sr] * sin_q).T.astype(BF16)
        ka_ref[hd] = (kv[:, sl] + krope + ref_col).astype(BF16)
        va_ref[hd, 0] = (kv[:, sr] + ones_col).T.astype(BF16)
    gq_cos = cg_ref[0:1, :] * cos_c
    gq_sin = cg_ref[1:2, :] * sin_c
    gk_cos = cg_ref[2:3, :] * cos_c
    gk_sin = cg_ref[3:4, :] * sin_c
    for hd in range(C_HEADS):
        sl = slice(hd * LANES, (hd + 1) * LANES)
        q = z[:, _C_QC[0] + hd * LANES:_C_QC[0] + (hd + 1) * LANES]
        qr = z[:, _C_QCR[0] + hd * LANES:_C_QCR[0] + (hd + 1) * LANES]
        inv = lax.rsqrt(jnp.sum(q * q, axis=-1, keepdims=True) * (1.0 / C_HD) + NORM_EPS)
        qc_ref[hd, 0] = ((q * gq_cos + qr * gq_sin) * inv).T.astype(BF16)
    for hd in range(C_KV_HEADS):
        sl = slice(hd * LANES, (hd + 1) * LANES)
        k = z[:, _C_KC[0] + hd * LANES:_C_KC[0] + (hd + 1) * LANES]
        kr = z[:, _C_KCR[0] + hd * LANES:_C_KCR[0] + (hd + 1) * LANES]
        inv = lax.rsqrt(jnp.sum(k * k, axis=-1, keepdims=True) * (1.0 / C_HD) + NORM_EPS)
        kc_ref[hd] = ((k * gk_cos + kr * gk_sin) * inv + ref_col).astype(BF16)
        vc_ref[hd, 0] = (z[:, _C_VC[0] + hd * LANES:_C_VC[0] + (hd + 1) * LANES] + ones_col).T.astype(BF16)


def _prep_ac(x, gmix, w1, gq, gkv, wq2, wkv2, cg, tabs, *, seq, tm):
    rows = x.shape[0]
    tiles_per_seq = seq // tm
    full = _resident
    row_spec = lambda w: pl.BlockSpec((tm, w), lambda i: (i, 0))
    nb = rows // seq
    heads = (A_HEADS, A_HEADS, A_HEADS, C_HEADS, C_KV_HEADS, C_KV_HEADS)
    transposed = (True, False, True, True, False, True)
    head_spec = lambda h: pl.BlockSpec((None, h, tm, LANES), lambda i: (i // tiles_per_seq, 0, i % tiles_per_seq, 0))
    slab_spec = lambda h: pl.BlockSpec((None, h, 1, LANES, tm), lambda i: (i // tiles_per_seq, 0, i % tiles_per_seq, 0, 0))
    return pl.pallas_call(
        _prep_ac_body,
        grid=(rows // tm,),
        in_specs=[row_spec(D_MODEL), full(gmix), full(w1), full(gq), full(gkv), full(wq2), full(wkv2), full(cg),
                  pl.BlockSpec((6, tm, LANES), lambda i: (0, i % tiles_per_seq, 0))],
        out_specs=[slab_spec(h) if t else head_spec(h) for h, t in zip(heads, transposed)],
        out_shape=[jax.ShapeDtypeStruct((nb, h, seq // tm, LANES, tm) if t else (nb, h, seq, LANES), BF16)
                   for h, t in zip(heads, transposed)],
        compiler_params=_params("arbitrary"),
        name="prep_ac",
    )(x, gmix, w1, gq, gkv, wq2, wkv2, cg, tabs)


def _proj_b_body(x_ref, g_ref, w_ref, o0_ref, o1_ref, o2_ref, z_ref):
    h = _rms(x_ref[...], g_ref[...]).astype(BF16)
    z = jnp.dot(h, w_ref[...], preferred_element_type=F32)
    tm = x_ref.shape[0]
    n_lt = z_ref.shape[0]
    for c in range(n_lt):
        z_ref[c] = z[:, c * LANES:(c + 1) * LANES]
    lt_per_group = n_lt // B_GROUPS
    for g, o_ref in enumerate((o0_ref, o1_ref, o2_ref)):
        d = B_PAIRS[g][1]
        for r in range(d):
            for c in range(lt_per_group):
                piece = z_ref[g * lt_per_group + c, pl.ds(r, tm // d, stride=d), :]
                o_ref[r, :, c * LANES:(c + 1) * LANES] = piece.astype(o_ref.dtype)


def _proj_b(x, g, w, *, nb, seq, tm):
    rows = x.shape[0]
    tiles_per_seq = seq // tm
    gw = 3 * B_HEADS * B_HD
    out_specs, out_shapes = [], []
    for _, d in B_PAIRS:
        out_specs.append(pl.BlockSpec((None, d, tm // d, gw), lambda i: (i // tiles_per_seq, 0, i % tiles_per_seq, 0)))
        out_shapes.append(jax.ShapeDtypeStruct((nb, d, seq // d, gw), BF16))
    return pl.pallas_call(
        _proj_b_body,
        grid=(rows // tm,),
        in_specs=[pl.BlockSpec((tm, D_MODEL), lambda i: (i, 0)), _resident(g), _resident(w)],
        out_specs=out_specs,
        out_shape=out_shapes,
        scratch_shapes=[pltpu.VMEM((w.shape[1] // LANES, tm, LANES), F32)],
        compiler_params=_params("arbitrary"),
        name="proj_b",
    )(x, g, w)


def _flash_body(q_ref, k_ref, vT_ref, o_ref, qT_ref, qx_ref, ksq_ref, acc_ref, m_ref, s0_ref, s1_ref,
                mc0_ref, mc1_ref, p0_ref, p1_ref, al0_ref, al1_ref, *, tk, n_chunks, rep):
    tq = o_ref.shape[0]
    kh = vT_ref.shape[2]
    assert tk == kh
    n_kh = vT_ref.shape[0]

    @pl.when((pl.program_id(2) == 0) & (pl.program_id(1) % rep == 0))
    def _():
        def norms(n, ksq):
            off = pl.multiple_of(n * kh, kh)
            kk = k_ref[pl.ds(off, kh), :].astype(F32)
            return jnp.maximum(ksq, jnp.sum(kk * kk, axis=1, keepdims=True))
        ksq = lax.fori_loop(0, n_kh, norms, jnp.zeros((kh, 1), F32))
        ksq_ref[...] = jnp.full(ksq_ref.shape, jnp.max(ksq), F32)

    qT = jnp.concatenate([q_ref[j] for j in range(q_ref.shape[0])], axis=1).astype(F32)
    bound = jnp.sqrt(jnp.sum(qT * qT, axis=0, keepdims=True) * ksq_ref[...]) * 1.01 + 1.0
    safe = jnp.max(bound) <= FLASH_SAFE_BITS
    row = lax.broadcasted_iota(jnp.int32, (LANES, tq), 0)
    qT_ref[...] = qT.astype(BF16)
    qx_ref[...] = jnp.where(row == LANES - 1, FLASH_REF_OFFSET - bound, qT).astype(BF16)
    acc_ref[...] = jnp.zeros(acc_ref.shape, F32)

    gk = FAST_SLABS * kh
    n_fast = n_kh // FAST_SLABS

    def fast_scores(n, p_ref):
        off = pl.multiple_of(n * gk, gk)
        s = jnp.dot(k_ref[pl.ds(off, gk), :], qx_ref[...], preferred_element_type=F32)
        p_ref[...] = jnp.exp2(s).astype(BF16)

    def fast_values(n, p_ref):
        vT = jnp.concatenate([vT_ref[FAST_SLABS * n + j, 0:V_ROWS, :] for j in range(FAST_SLABS)], axis=1)
        acc_ref[0:V_ROWS, :] += jnp.dot(vT, p_ref[...], preferred_element_type=F32)

    @pl.when(safe)
    def _():
        fast_scores(0, p0_ref)

        def quad(n0, last):
            fast_scores(n0 + 1, p1_ref)
            fast_values(n0, p0_ref)
            fast_scores(n0 + 2, p0_ref)
            fast_values(n0 + 1, p1_ref)
            fast_scores(n0 + 3, p1_ref)
            fast_values(n0 + 2, p0_ref)
            if not last:
                fast_scores(n0 + 4, p0_ref)
            fast_values(n0 + 3, p1_ref)

        def body(j, carry):
            quad(4 * j, False)
            return carry

        lax.fori_loop(0, n_fast // 4 - 1, body, 0)
        quad(n_fast - 4, True)

    @pl.when(jnp.logical_not(safe))
    def _():
        _flash_running_max(k_ref, qT_ref, vT_ref, acc_ref, m_ref, s0_ref, s1_ref, mc0_ref, mc1_ref,
                           p0_ref, p1_ref, al0_ref, al1_ref, tk=tk, n_chunks=n_chunks, tq=tq)

    acc = acc_ref[...]
    o_ref[...] = (acc * (1.0 / acc[A_V:A_V + 1, :])).T.astype(o_ref.dtype)


def _flash_running_max(k_ref, qT_ref, vT_ref, acc_ref, m_ref, s0_ref, s1_ref, mc0_ref, mc1_ref,
                       p0_ref, p1_ref, al0_ref, al1_ref, *, tk, n_chunks, tq):
    rg = FLASH_ROW_GROUP
    n_groups = tk // rg
    m_ref[...] = jnp.full(m_ref.shape, NEG_BIG, F32)

    def stage_a(c, g, s_ref, mc_ref):
        off = pl.multiple_of(c * tk + g * rg, rg)
        sg = jnp.dot(k_ref[pl.ds(off, rg), :], qT_ref[...], preferred_element_type=F32)
        s_ref[g * rg:(g + 1) * rg, :] = sg
        part = jnp.max(sg.reshape(rg // 8, 8, tq), axis=0)
        mc_ref[...] = part if g == 0 else jnp.maximum(mc_ref[...], part)

    def stage_b_head(mc_ref, al_ref):
        m_prev = m_ref[...]
        m_new = jnp.maximum(m_prev, jnp.max(mc_ref[...], axis=0, keepdims=True))
        al_ref[...] = jnp.exp2(m_prev - m_new)
        m_ref[...] = m_new

    def stage_b(g, s_ref, p_ref):
        rows = slice(g * rg, (g + 1) * rg)
        p_ref[rows, :] = jnp.exp2(s_ref[rows, :] - m_ref[...]).astype(BF16)

    def stage_c(c, p_ref, al_ref):
        pv = jnp.dot(vT_ref[c, 0:V_ROWS, :], p_ref[0:tk, :], preferred_element_type=F32)
        acc_ref[0:V_ROWS, :] = acc_ref[0:V_ROWS, :] * al_ref[...] + pv

    def step(c_a, c_c, s_in, mc_in, s_out, mc_out, p_in, al_in, p_out, al_out):
        stage_b_head(mc_in, al_out)
        stage_c(c_c, p_in, al_in)
        for g in range(n_groups):
            stage_a(c_a, g, s_out, mc_out)
            stage_b(g, s_in, p_out)

    for g in range(n_groups):
        stage_a(0, g, s0_ref, mc0_ref)
    stage_b_head(mc0_ref, al0_ref)
    for g in range(n_groups):
        stage_a(1, g, s1_ref, mc1_ref)
        stage_b(g, s0_ref, p0_ref)

    def pair(j, carry):
        c0 = 2 * j
        step(jnp.minimum(c0 + 2, n_chunks - 1), c0, s1_ref, mc1_ref, s0_ref, mc0_ref, p0_ref, al0_ref, p1_ref, al1_ref)
        step(jnp.minimum(c0 + 3, n_chunks - 1), c0 + 1, s0_ref, mc0_ref, s1_ref, mc1_ref, p1_ref, al1_ref, p0_ref, al0_ref)
        return carry

    lax.fori_loop(0, n_chunks // 2, pair, 0)


def _flash(q, k, vT, *, rep, tq, tk, name):
    nb, hq = q.shape[:2]
    seq = k.shape[2]
    assert vT.shape[2:] == (seq // tk, LANES, tk) and q.shape[2:] == (seq // tk, LANES, tk)
    n_chunks = seq // tk
    kh = tk
    assert n_chunks % (4 * FAST_SLABS) == 0
    body = functools.partial(_flash_body, tk=tk, n_chunks=n_chunks, rep=rep)
    return pl.pallas_call(
        body,
        grid=(nb, hq, seq // tq),
        in_specs=[pl.BlockSpec((None, None, tq // tk, LANES, tk), lambda b, h, i: (b, h, i, 0, 0)),
                  pl.BlockSpec((None, None, seq, LANES), lambda b, h, i: (b, h // rep, 0, 0)),
                  pl.BlockSpec((None, None, seq // kh, LANES, kh), lambda b, h, i: (b, h // rep, 0, 0, 0))],
        out_specs=pl.BlockSpec((None, None, tq, LANES), lambda b, h, i: (b, h, i, 0)),
        out_shape=jax.ShapeDtypeStruct((nb, hq, seq, LANES), BF16),
        scratch_shapes=[pltpu.VMEM((LANES, tq), BF16), pltpu.VMEM((LANES, tq), BF16),
                        pltpu.VMEM((1, tq), F32),
                        pltpu.VMEM((LANES, tq), F32), pltpu.VMEM((1, tq), F32),
                        pltpu.VMEM((tk, tq), F32), pltpu.VMEM((tk, tq), F32),
                        pltpu.VMEM((8, tq), F32), pltpu.VMEM((8, tq), F32),
                        pltpu.VMEM((FAST_SLABS * tk, tq), BF16), pltpu.VMEM((FAST_SLABS * tk, tq), BF16),
                        pltpu.VMEM((1, tq), F32), pltpu.VMEM((1, tq), F32)],
        compiler_params=_params("arbitrary", "arbitrary", "arbitrary"),
        name=name,
    )(q, k, vT)


B_TU = 128
B_WIN = B_TU + 2 * B_SIDE
B_SUB = 4


def _dilated_body(q_ref, kp_ref, kc_ref, kn_ref, vp_ref, vc_ref, vn_ref, bias_ref, o_ref, lse_ref, *, n_u):
    u0 = pl.program_id(2) * (B_SUB * B_TU)
    kall = jnp.concatenate([kp_ref[B_TU - B_SIDE:, :], kc_ref[...], kn_ref[:B_SIDE, :]], axis=0)
    vall = jnp.concatenate([vp_ref[B_TU - B_SIDE:, :], vc_ref[...], vn_ref[:B_SIDE, :]], axis=0)
    col = lax.broadcasted_iota(jnp.int32, (2 * B_TU, B_WIN), 1)
    lane = lax.broadcasted_iota(jnp.int32, (B_TU, LANES), 1)
    low = lane < B_HD
    n_pairs = B_HEADS // 2
    lanes = [slice(hp * LANES, (hp + 1) * LANES) for hp in range(n_pairs)]
    units = [(t, hp) for t in range(B_SUB) for hp in range(n_pairs)]
    scores = []
    for t, hp in units:
        key_u = u0 + t * B_TU - B_SIDE + col
        valid = (key_u >= 0) & (key_u < n_u)
        qp = q_ref[t * B_TU:(t + 1) * B_TU, lanes[hp]]
        zero = jnp.zeros_like(qp)
        q2 = jnp.concatenate([jnp.where(low, qp, zero), jnp.where(low, zero, qp)], axis=0)
        kw = kall[t * B_TU:t * B_TU + B_WIN, lanes[hp]]
        s = lax.dot_general(q2, kw, (((1,), (1,)), ((), ())), preferred_element_type=F32)
        scores.append(jnp.where(valid, s + bias_ref[hp], NEG_BIG))
    maxes = [jnp.max(s, axis=-1, keepdims=True) for s in scores]
    probs = [jnp.exp2(s - m) for s, m in zip(scores, maxes)]
    sums = [jnp.sum(p, axis=-1, keepdims=True) for p in probs]
    outs = [jnp.dot(p.astype(BF16), vall[t * B_TU:t * B_TU + B_WIN, lanes[hp]], preferred_element_type=F32) * (1.0 / l)
            for (t, hp), p, l in zip(units, probs, sums)]
    for n, (t, hp) in enumerate(units):
        rows = slice(t * B_TU, (t + 1) * B_TU)
        lse = jnp.broadcast_to(maxes[n] + jnp.log2(sums[n]), (2 * B_TU, LANES))
        o_ref[rows, lanes[hp]] = jnp.where(low, outs[n][:B_TU], outs[n][B_TU:]).astype(o_ref.dtype)
        lse_ref[rows, lanes[hp]] = jnp.where(low, lse[:B_TU], lse[B_TU:])


def _dilated(zg, bias, *, name):
    nb, dilation, n_u, _ = zg.shape
    tile = B_SUB * B_TU
    nstep = n_u // tile
    nblk = n_u // B_TU
    hw = B_HEADS * B_HD

    def main(part):
        return pl.BlockSpec((None, None, tile, hw), lambda b, r, i: (b, r, i, part))

    def halo(part, before):
        def imap(b, r, i):
            blk = i * B_SUB - 1 if before else (i + 1) * B_SUB
            return (b, r, jnp.clip(blk, 0, nblk - 1), part)
        return pl.BlockSpec((None, None, B_TU, hw), imap)

    out_spec = pl.BlockSpec((None, None, tile, hw), lambda b, r, i: (b, r, i, 0))
    return pl.pallas_call(
        functools.partial(_dilated_body, n_u=n_u),
        grid=(nb, dilation, nstep),
        in_specs=[main(0), halo(1, True), main(1), halo(1, False), halo(2, True), main(2), halo(2, False),
                  pl.BlockSpec(bias.shape, lambda b, r, i: (0, 0, 0))],
        out_specs=[out_spec, out_spec],
        out_shape=[jax.ShapeDtypeStruct((nb, dilation, n_u, hw), BF16),
                   jax.ShapeDtypeStruct((nb, dilation, n_u, hw), F32)],
        compiler_params=_params("arbitrary", "arbitrary", "arbitrary"),
        name=name,
    )(zg, zg, zg, zg, zg, zg, zg, bias)


def _merge_body(x_ref, oa_ref, oc_ref, ob0_ref, ob1_ref, ob2_ref, l0_ref, l1_ref, l2_ref, gmix_ref, wg_ref,
                wa_ref, wb_ref, wc_ref, wo_ref, out_ref, on_ref, ln_ref):
    tm = x_ref.shape[0]
    x = x_ref[...]
    gate = jnp.dot(_rms(x, gmix_ref[...]).astype(BF16), wg_ref[...], preferred_element_type=F32)
    n_lt = on_ref.shape[1]
    for g, (o_ref, l_ref) in enumerate(((ob0_ref, l0_ref), (ob1_ref, l1_ref), (ob2_ref, l2_ref))):
        d = B_PAIRS[g][1]
        for r in range(d):
            for c in range(n_lt):
                cols = slice(c * LANES, (c + 1) * LANES)
                on_ref[g, c, pl.ds(r, tm // d, stride=d), :] = o_ref[r, :, cols].astype(F32)
                ln_ref[g, c, pl.ds(r, tm // d, stride=d), :] = l_ref[r, :, cols]
    wide = lambda ref, g: jnp.concatenate([ref[g, c] for c in range(n_lt)], axis=1)
    l0, l1, l2 = wide(ln_ref, 0), wide(ln_ref, 1), wide(ln_ref, 2)
    mx = jnp.maximum(jnp.maximum(l0, l1), l2)
    e0, e1, e2 = jnp.exp2(l0 - mx), jnp.exp2(l1 - mx), jnp.exp2(l2 - mx)
    ob = e0 * wide(on_ref, 0) + e1 * wide(on_ref, 1) + e2 * wide(on_ref, 2)
    ob = (ob * (1.0 / (e0 + e1 + e2))).astype(BF16)
    sg = 1.0 / (1.0 + jnp.exp(-gate))
    heads_to_lanes = lambda ref: jnp.concatenate([ref[h] for h in range(ref.shape[0])], axis=1)
    mix = sg[:, 0:D_MODEL] * jnp.dot(heads_to_lanes(oa_ref), wa_ref[...], preferred_element_type=F32)
    mix += sg[:, D_MODEL:2 * D_MODEL] * jnp.dot(ob, wb_ref[...], preferred_element_type=F32)
    mix += sg[:, 2 * D_MODEL:] * jnp.dot(heads_to_lanes(oc_ref), wc_ref[...], preferred_element_type=F32)
    out_ref[...] = x + jnp.dot(mix.astype(BF16), wo_ref[...], preferred_element_type=F32)


def _merge(x, oa, oc, obs, lses, gmix, wg, wa, wb, wc, wo, *, seq, tm):
    rows = x.shape[0]
    tiles_per_seq = seq // tm
    row_spec = lambda w: pl.BlockSpec((tm, w), lambda i: (i, 0))
    full = _resident
    hw = B_HEADS * B_HD
    res_specs = [pl.BlockSpec((None, d, tm // d, hw), lambda i: (i // tiles_per_seq, 0, i % tiles_per_seq, 0))
                 for _, d in B_PAIRS]
    head_spec = lambda o: pl.BlockSpec((None, o.shape[1], tm, LANES),
                                       lambda i: (i // tiles_per_seq, 0, i % tiles_per_seq, 0))
    return pl.pallas_call(
        _merge_body,
        grid=(rows // tm,),
        in_specs=[row_spec(D_MODEL), head_spec(oa), head_spec(oc)] + res_specs + res_specs
                 + [full(gmix), full(wg), full(wa), full(wb), full(wc), full(wo)],
        out_specs=row_spec(D_MODEL),
        out_shape=jax.ShapeDtypeStruct((rows, D_MODEL), F32),
        scratch_shapes=[pltpu.VMEM((B_GROUPS, hw // LANES, tm, LANES), F32),
                        pltpu.VMEM((B_GROUPS, hw // LANES, tm, LANES), F32)],
        compiler_params=_params("arbitrary"),
        name="merge_out",
    )(x, oa, oc, *obs, *lses, gmix, wg, wa, wb, wc, wo)


FFN_CHUNK = 1024


def _ffn_body(x_ref, g_ref, wup_ref, wdn_ref, gfin_ref, out_ref, *, final):
    x = x_ref[...]
    h = _rms(x, g_ref[...]).astype(BF16)
    acc = x
    for c in range(D_FF // FFN_CHUNK):
        cs = slice(c * FFN_CHUNK, (c + 1) * FFN_CHUNK)
        u = jnp.dot(h, wup_ref[:, cs], preferred_element_type=F32)
        a = jnp.square(jnp.maximum(u, 0.0)).astype(BF16)
        acc = acc + jnp.dot(a, wdn_ref[cs, :], preferred_element_type=F32)
    if final:
        acc = _rms(acc, gfin_ref[...])
    out_ref[...] = acc


def _ffn(x, g, wup, wdn, gfin, *, final, tm):
    rows = x.shape[0]
    full = _resident
    return pl.pallas_call(
        functools.partial(_ffn_body, final=final),
        grid=(rows // tm,),
        in_specs=[pl.BlockSpec((tm, D_MODEL), lambda i: (i, 0)), full(g), full(wup), full(wdn), full(gfin)],
        out_specs=pl.BlockSpec((tm, D_MODEL), lambda i: (i, 0)),
        out_shape=jax.ShapeDtypeStruct((rows, D_MODEL), F32),
        compiler_params=_params("arbitrary"),
        name="ffn_final" if final else "ffn",
    )(x, g, wup, wdn, gfin)


def _pad_heads(w, n_heads, hd, lane_off=0):
    k = w.shape[0]
    w = w.reshape(k, n_heads, hd)
    w = jnp.pad(w, ((0, 0), (0, 0), (lane_off, LANES - lane_off - hd)))
    return w.reshape(k, n_heads * LANES)


def _pad_head_rows(w, n_heads, hd):
    n = w.shape[1]
    return jnp.pad(w.reshape(n_heads, hd, n), ((0, 0), (0, LANES - hd), (0, 0))).reshape(n_heads * LANES, n)


def _rot_cols(w):
    half = w.shape[-1] // 2
    return jnp.concatenate([-w[..., half:], w[..., :half]], axis=-1)


def _rot_axial(w):
    half = C_HD // 2
    return jnp.concatenate([_rot_cols(w[..., :half]), _rot_cols(w[..., half:])], axis=-1)


def _swap_axial(g):
    q = C_HD // 4
    return jnp.concatenate([g[q:2 * q], g[:q], g[3 * q:], g[2 * q:3 * q]], axis=-1)


def _rope_angles(pos, half):
    freqs = ROPE_THETA ** (-jnp.arange(half, dtype=F32) / half)
    return pos.astype(F32)[:, None] * freqs[None, :]


def _tables(seq):
    pos = jnp.arange(seq, dtype=jnp.int32)
    ang = _rope_angles(pos, A_ROPE // 2)
    cos_r = jnp.tile(jnp.cos(ang), (1, 2))
    sin_r = jnp.tile(jnp.sin(ang), (1, 2))
    zeros = lambda w: jnp.zeros((seq, w), F32)
    cos_k = jnp.concatenate([zeros(A_NOPE), cos_r, zeros(LANES - A_NOPE - A_ROPE)], axis=1)
    sin_k = jnp.concatenate([zeros(A_NOPE), sin_r, zeros(LANES - A_NOPE - A_ROPE)], axis=1)
    scale_a = (A_NOPE + A_ROPE) ** -0.5 * LOG2_E
    nope = jnp.concatenate([jnp.ones((seq, A_NOPE), F32), zeros(LANES - A_NOPE)], axis=1)
    cos_q = (nope + cos_k) * scale_a
    sin_q = sin_k * scale_a
    quarter = C_HD // 4
    ang_r = _rope_angles(pos // GRID_W, quarter)
    ang_c = _rope_angles(pos % GRID_W, quarter)
    cos_c = jnp.concatenate([jnp.tile(jnp.cos(ang_r), (1, 2)), jnp.tile(jnp.cos(ang_c), (1, 2)), zeros(LANES - C_HD)], axis=1)
    sin_c = jnp.concatenate([jnp.tile(jnp.sin(ang_r), (1, 2)), jnp.tile(jnp.sin(ang_c), (1, 2)), zeros(LANES - C_HD)], axis=1)
    return jnp.stack([cos_q, sin_q, cos_k, sin_k, cos_c, sin_c], axis=0)


def _t5_bucket(rel):
    half = NUM_BUCKETS // 2
    max_exact = half // 2
    ret = jnp.where(rel > 0, half, 0)
    n = jnp.abs(rel)
    nf = jnp.maximum(n, 1).astype(F32)
    large = max_exact + (jnp.log(nf / max_exact) / math.log(MAX_DISTANCE / max_exact) * (half - max_exact)).astype(jnp.int32)
    large = jnp.minimum(large, half - 1)
    return ret + jnp.where(n < max_exact, n, large)


def _band_bias(t5_table, group, dilation):
    offs = dilation * jnp.arange(-B_SIDE, B_SIDE + 1, dtype=jnp.int32)
    bias_hj = t5_table[_t5_bucket(offs)][:, group * B_HEADS:(group + 1) * B_HEADS].T.astype(F32) * LOG2_E
    period = B_WIN + B_TU + 1
    neg = jnp.full((B_HEADS, B_TU), NEG_BIG, F32)
    padded = jnp.concatenate([neg, bias_hj, neg], axis=1)
    flow = jnp.tile(padded, (1, B_TU + 1))[:, :B_TU * (period - 1)].reshape(B_HEADS, B_TU, period - 1)
    tiles = flow[:, :, B_TU:]
    return tiles.reshape(B_HEADS // 2, 2 * B_TU, B_WIN)


def _layer_weights(l, norm_mix, w_in, a_q_norm, a_kv_norm, a_w_uq, a_w_ukv, c_q_norm, c_k_norm,
                   w_br_a, w_br_b, w_br_c, w_out, norm_ffn, w_up, w_down):
    w = w_in[l]
    o = 0
    cols = []
    for width in (A_Q_LORA, A_KV_LORA, A_ROPE, 3 * B_GROUPS * B_HEADS * B_HD, C_HEADS * C_HD,
                  C_KV_HEADS * C_HD, C_KV_HEADS * C_HD, N_BRANCH * D_MODEL):
        cols.append(w[:, o:o + width])
        o += width
    w_cq, w_ckv, w_kr, w_b, w_qc, w_kc, w_vc, w_gate = cols
    kr_p = _pad_heads(w_kr, 1, A_ROPE, A_NOPE)
    kr_rot_p = _pad_heads(_rot_cols(w_kr), 1, A_ROPE, A_NOPE)
    qc = w_qc.reshape(D_MODEL, C_HEADS, C_HD)
    kc = w_kc.reshape(D_MODEL, C_KV_HEADS, C_HD)
    w1 = jnp.concatenate([
        w_cq, w_ckv, kr_p, kr_rot_p,
        _pad_heads(w_qc, C_HEADS, C_HD), _pad_heads(_rot_axial(qc).reshape(D_MODEL, -1), C_HEADS, C_HD),
        _pad_heads(w_kc, C_KV_HEADS, C_HD), _pad_heads(_rot_axial(kc).reshape(D_MODEL, -1), C_KV_HEADS, C_HD),
        _pad_heads(w_vc, C_KV_HEADS, C_HD)], axis=1).astype(BF16)
    uq = a_w_uq[l].reshape(A_Q_LORA, A_HEADS, A_NOPE + A_ROPE)
    uq_rot = jnp.concatenate([jnp.zeros((A_Q_LORA, A_HEADS, A_NOPE), F32), _rot_cols(uq[..., A_NOPE:])], axis=-1)
    wq2 = jnp.concatenate([_pad_heads(uq.reshape(A_Q_LORA, -1), A_HEADS, A_NOPE + A_ROPE),
                           _pad_heads(uq_rot.reshape(A_Q_LORA, -1), A_HEADS, A_NOPE + A_ROPE)], axis=1).astype(BF16)
    ukv = a_w_ukv[l].reshape(A_KV_LORA, A_HEADS, A_NOPE + A_V)
    wkv2 = jnp.concatenate([_pad_heads(ukv[..., :A_NOPE].reshape(A_KV_LORA, -1), A_HEADS, A_NOPE),
                            _pad_heads(ukv[..., A_NOPE:].reshape(A_KV_LORA, -1), A_HEADS, A_V)], axis=1).astype(BF16)
    pad_gain = lambda g: jnp.pad(g, (0, LANES - C_HD))
    scale_c = C_HD ** -0.5 * LOG2_E
    cg = jnp.stack([pad_gain(c_q_norm[l]) * scale_c, pad_gain(_swap_axial(c_q_norm[l])) * scale_c,
                    pad_gain(c_k_norm[l]), pad_gain(_swap_axial(c_k_norm[l]))], axis=0)
    wb5 = w_b.reshape(D_MODEL, 3, B_GROUPS, B_HEADS * B_HD)
    wb5 = wb5 * jnp.array([B_HD ** -0.5 * LOG2_E, 1.0, 1.0], F32)[None, :, None, None]
    w_bq = jnp.transpose(wb5, (0, 2, 1, 3)).reshape(D_MODEL, -1).astype(BF16)
    return dict(
        gmix=norm_mix[l][None, :], w1=w1, gq=a_q_norm[l][None, :], gkv=a_kv_norm[l][None, :], wq2=wq2, wkv2=wkv2,
        cg=cg, w_bq=w_bq, w_gate=w_gate.astype(BF16),
        wa=_pad_head_rows(w_br_a[l], A_HEADS, A_V).astype(BF16), wb=w_br_b[l].astype(BF16),
        wc=_pad_head_rows(w_br_c[l], C_HEADS, C_HD).astype(BF16), wo=w_out[l].astype(BF16),
        gffn=norm_ffn[l][None, :], wup=w_up[l].astype(BF16), wdn=w_down[l].astype(BF16))


def _encoder_layer(x, wts, tabs, biases, *, nb, seq, final, final_norm, tm, tq, tk):
    qa, ka, va, qc, kc, vc = _prep_ac(x, wts["gmix"], wts["w1"], wts["gq"], wts["gkv"], wts["wq2"], wts["wkv2"],
                                      wts["cg"], tabs, seq=seq, tm=tm)
    zbs = _proj_b(x, wts["gmix"], wts["w_bq"], nb=nb, seq=seq, tm=tm)

    assert tm == tk
    oa = _flash(qa, ka, va, rep=1, tq=tq, tk=tk, name="flash_a")
    oc = _flash(qc, kc, vc, rep=C_HEADS // C_KV_HEADS, tq=tq, tk=tk, name="flash_c")

    obs, lses = [], []
    for g in range(B_GROUPS):
        o_g, lse_g = _dilated(zbs[g], biases[g], name=f"dilated_{g}")
        obs.append(o_g)
        lses.append(lse_g)

    x = _merge(x, oa, oc, obs, lses, wts["gmix"], wts["w_gate"], wts["wa"], wts["wb"], wts["wc"], wts["wo"],
               seq=seq, tm=tm)
    return _ffn(x, wts["gffn"], wts["wup"], wts["wdn"], final_norm, final=final, tm=tm)


def _trunk(x, norm_mix, w_in, a_q_norm, a_kv_norm, a_w_uq, a_w_ukv, c_q_norm, c_k_norm,
           w_br_a, w_br_b, w_br_c, w_out, norm_ffn, w_up, w_down, t5_table, final_norm, *, tm, tq, tk):
    nb, seq, _ = x.shape
    tabs = _tables(seq)
    biases = [_band_bias(t5_table, g, d) for g, (_, d) in enumerate(B_PAIRS)]
    depth = w_in.shape[0]
    xr = x.reshape(nb * seq, D_MODEL)
    for l in range(depth):
        wts = _layer_weights(l, norm_mix, w_in, a_q_norm, a_kv_norm, a_w_uq, a_w_ukv, c_q_norm, c_k_norm,
                             w_br_a, w_br_b, w_br_c, w_out, norm_ffn, w_up, w_down)
        xr = _encoder_layer(xr, wts, tabs, biases, nb=nb, seq=seq, final=(l == depth - 1),
                            final_norm=final_norm[None, :], tm=tm, tq=tq, tk=tk)
    return xr.reshape(nb, seq, D_MODEL)


def kernel(x_prompt, x_sample, norm_mix, w_in, a_q_norm, a_kv_norm, a_w_uq, a_w_ukv, c_q_norm, c_k_norm,
           w_br_a, w_br_b, w_br_c, w_out, norm_ffn, w_up, w_down, t5_table, final_norm):
    assert x_prompt.shape[1:] == x_sample.shape[1:]
    n_prompt = x_prompt.shape[0]
    x = jnp.concatenate([x_prompt, x_sample], axis=0)
    y = _trunk(x, norm_mix, w_in, a_q_norm, a_kv_norm, a_w_uq, a_w_ukv, c_q_norm, c_k_norm,
               w_br_a, w_br_b, w_br_c, w_out, norm_ffn, w_up, w_down, t5_table, final_norm,
               tm=512, tq=2048, tk=512)
    return (y[:n_prompt], y[n_prompt:])
```

```python
import functools
import math

import jax
import jax.numpy as jnp
from jax import lax
from jax.experimental import pallas as pl
from jax.experimental.pallas import tpu as pltpu

D_MODEL = 1024
GRID_W = 64
NORM_EPS = 1e-6
ROPE_THETA = 10000.0
NEG_BIG = -1e30
A_HEADS, A_NOPE, A_ROPE, A_V = 8, 64, 32, 64
A_Q_LORA, A_KV_LORA = 384, 256
B_PAIRS = ((128, 1), (512, 4), (2048, 16))
B_GROUPS, B_HEADS, B_HD = 3, 8, 64
C_HEADS, C_KV_HEADS, C_HD = 8, 2, 64
NUM_BUCKETS, MAX_DISTANCE = 32, 2048
D_FF = 4 * D_MODEL
N_BRANCH = 3
B_SIDE = 64

LANES = 128
VMEM_LIMIT = 48 * 1024 * 1024
FAST_SLABS = 2
FLASH_ROW_GROUP = 128
FLASH_SAFE_BITS = 64.0
FLASH_REF_OFFSET = 30.0

LOG2_E = math.log2(math.e)

BF16 = jnp.bfloat16
F32 = jnp.float32


def _params(*sem):
    return pltpu.CompilerParams(dimension_semantics=sem, vmem_limit_bytes=VMEM_LIMIT)


def _resident(a):
    return pl.BlockSpec(a.shape, lambda i: (0,) * a.ndim, pipeline_mode=pl.Buffered(1))


def _rms(x, g):
    return x * lax.rsqrt(jnp.mean(x * x, axis=-1, keepdims=True) + NORM_EPS) * g


_C_CQ = (0, 384)
_C_CKV = (384, 640)
_C_KR = (640, 768)
_C_KRR = (768, 896)
_C_QC = (896, 1920)
_C_QCR = (1920, 2944)
_C_KC = (2944, 3200)
_C_KCR = (3200, 3456)
_C_VC = (3456, 3712)


def _prep_ac_body(x_ref, gmix_ref, w1_ref, gq_ref, gkv_ref, wq2_ref, wkv2_ref, cg_ref, tab_ref,
                  qa_ref, ka_ref, va_ref, qc_ref, kc_ref, vc_ref):
    h = _rms(x_ref[...], gmix_ref[...]).astype(BF16)
    z = jnp.dot(h, w1_ref[...], preferred_element_type=F32)
    cqn = _rms(z[:, _C_CQ[0]:_C_CQ[1]], gq_ref[...]).astype(BF16)
    ckvn = _rms(z[:, _C_CKV[0]:_C_CKV[1]], gkv_ref[...]).astype(BF16)
    qq = jnp.dot(cqn, wq2_ref[...], preferred_element_type=F32)
    kv = jnp.dot(ckvn, wkv2_ref[...], preferred_element_type=F32)
    cos_q, sin_q, cos_k, sin_k, cos_c, sin_c = (tab_ref[t] for t in range(6))
    krope = z[:, _C_KR[0]:_C_KR[1]] * cos_k + z[:, _C_KRR[0]:_C_KRR[1]] * sin_k
    lane = lax.broadcasted_iota(jnp.int32, (1, LANES), 1)
    ones_col = (lane == A_V).astype(F32)
    ref_col = (lane == LANES - 1).astype(F32)
    hw = A_HEADS * LANES
    for hd in range(A_HEADS):
        sl = slice(hd * LANES, (hd + 1) * LANES)
        sr = slice(hw + hd * LANES, hw + (hd + 1) * LANES)
        qa_ref[hd, 0] = (qq[:, sl] * cos_q + qq[:, sr] * sin_q).T.astype(BF16)
        ka_ref[hd] = (kv[:, sl] + krope + ref_col).astype(BF16)
        va_ref[hd, 0] = (kv[:, sr] + ones_col).T.astype(BF16)
    gq_cos = cg_ref[0:1, :] * cos_c
    gq_sin = cg_ref[1:2, :] * sin_c
    gk_cos = cg_ref[2:3, :] * cos_c
    gk_sin = cg_ref[3:4, :] * sin_c
    for hd in range(C_HEADS):
        sl = slice(hd * LANES, (hd + 1) * LANES)
        q = z[:, _C_QC[0] + hd * LANES:_C_QC[0] + (hd + 1) * LANES]
        qr = z[:, _C_QCR[0] + hd * LANES:_C_QCR[0] + (hd + 1) * LANES]
        inv = lax.rsqrt(jnp.sum(q * q, axis=-1, keepdims=True) * (1.0 / C_HD) + NORM_EPS)
        qc_ref[hd, 0] = ((q * gq_cos + qr * gq_sin) * inv).T.astype(BF16)
    for hd in range(C_KV_HEADS):
        sl = slice(hd * LANES, (hd + 1) * LANES)
        k = z[:, _C_KC[0] + hd * LANES:_C_KC[0] + (hd + 1) * LANES]
        kr = z[:, _C_KCR[0] + hd * LANES:_C_KCR[0] + (hd + 1) * LANES]
        inv = lax.rsqrt(jnp.sum(k * k, axis=-1, keepdims=True) * (1.0 / C_HD) + NORM_EPS)
        kc_ref[hd] = ((k * gk_cos + kr * gk_sin) * inv + ref_col).astype(BF16)
        vc_ref[hd, 0] = (z[:, _C_VC[0] + hd * LANES:_C_VC[0] + (hd + 1) * LANES] + ones_col).T.astype(BF16)


def _prep_ac(x, gmix, w1, gq, gkv, wq2, wkv2, cg, tabs, *, seq, tm):
    rows = x.shape[0]
    tiles_per_seq = seq // tm
    full = _resident
    row_spec = lambda w: pl.BlockSpec((tm, w), lambda i: (i, 0))
    nb = rows // seq
    heads = (A_HEADS, A_HEADS, A_HEADS, C_HEADS, C_KV_HEADS, C_KV_HEADS)
    transposed = (True, False, True, True, False, True)
    head_spec = lambda h: pl.BlockSpec((None, h, tm, LANES), lambda i: (i // tiles_per_seq, 0, i % tiles_per_seq, 0))
    slab_spec = lambda h: pl.BlockSpec((None, h, 1, LANES, tm), lambda i: (i // tiles_per_seq, 0, i % tiles_per_seq, 0, 0))
    return pl.pallas_call(
        _prep_ac_body,
        grid=(rows // tm,),
        in_specs=[row_spec(D_MODEL), full(gmix), full(w1), full(gq), full(gkv), full(wq2), full(wkv2), full(cg),
                  pl.BlockSpec((6, tm, LANES), lambda i: (0, i % tiles_per_seq, 0))],
        out_specs=[slab_spec(h) if t else head_spec(h) for h, t in zip(heads, transposed)],
        out_shape=[jax.ShapeDtypeStruct((nb, h, seq // tm, LANES, tm) if t else (nb, h, seq, LANES), BF16)
                   for h, t in zip(heads, transposed)],
        compiler_params=_params("arbitrary"),
        name="prep_ac",
    )(x, gmix, w1, gq, gkv, wq2, wkv2, cg, tabs)


def _proj_b_body(x_ref, g_ref, w_ref, o0_ref, o1_ref, o2_ref, z_ref):
    h = _rms(x_ref[...], g_ref[...]).astype(BF16)
    z = jnp.dot(h, w_ref[...], preferred_element_type=F32)
    tm = x_ref.shape[0]
    n_lt = z_ref.shape[0]
    for c in range(n_lt):
        z_ref[c] = z[:, c * LANES:(c + 1) * LANES]
    lt_per_group = n_lt // B_GROUPS
    for g, o_ref in enumerate((o0_ref, o1_ref, o2_ref)):
        d = B_PAIRS[g][1]
        for r in range(d):
            for c in range(lt_per_group):
                piece = z_ref[g * lt_per_group + c, pl.ds(r, tm // d, stride=d), :]
                o_ref[r, :, c * LANES:(c + 1) * LANES] = piece.astype(o_ref.dtype)


def _proj_b(x, g, w, *, nb, seq, tm):
    rows = x.shape[0]
    tiles_per_seq = seq // tm
    gw = 3 * B_HEADS * B_HD
    out_specs, out_shapes = [], []
    for _, d in B_PAIRS:
        out_specs.append(pl.BlockSpec((None, d, tm // d, gw), lambda i: (i // tiles_per_seq, 0, i % tiles_per_seq, 0)))
        out_shapes.append(jax.ShapeDtypeStruct((nb, d, seq // d, gw), BF16))
    return pl.pallas_call(
        _proj_b_body,
        grid=(rows // tm,),
        in_specs=[pl.BlockSpec((tm, D_MODEL), lambda i: (i, 0)), _resident(g), _resident(w)],
        out_specs=out_specs,
        out_shape=out_shapes,
        scratch_shapes=[pltpu.VMEM((w.shape[1] // LANES, tm, LANES), F32)],
        compiler_params=_params("arbitrary"),
        name="proj_b",
    )(x, g, w)


def _flash_body(q_ref, k_ref, vT_ref, o_ref, qT_ref, qx_ref, ksq_ref, acc_ref, m_ref, s0_ref, s1_ref,
                mc0_ref, mc1_ref, p0_ref, p1_ref, al0_ref, al1_ref, *, tk, n_chunks, rep):
    tq = o_ref.shape[0]
    kh = vT_ref.shape[2]
    assert tk == kh
    n_kh = vT_ref.shape[0]

    @pl.when((pl.program_id(2) == 0) & (pl.program_id(1) % rep == 0))
    def _():
        def norms(n, ksq):
            off = pl.multiple_of(n * kh, kh)
            kk = k_ref[pl.ds(off, kh), :].astype(F32)
            return jnp.maximum(ksq, jnp.sum(kk * kk, axis=1, keepdims=True))
        ksq = lax.fori_loop(0, n_kh, norms, jnp.zeros((kh, 1), F32))
        ksq_ref[...] = jnp.full(ksq_ref.shape, jnp.max(ksq), F32)

    qT = jnp.concatenate([q_ref[j] for j in range(q_ref.shape[0])], axis=1).astype(F32)
    bound = jnp.sqrt(jnp.sum(qT * qT, axis=0, keepdims=True) * ksq_ref[...]) * 1.01 + 1.0
    safe = jnp.max(bound) <= FLASH_SAFE_BITS
    row = lax.broadcasted_iota(jnp.int32, (LANES, tq), 0)
    qT_ref[...] = qT.astype(BF16)
    qx_ref[...] = jnp.where(row == LANES - 1, FLASH_REF_OFFSET - bound, qT).astype(BF16)
    acc_ref[...] = jnp.zeros(acc_ref.shape, F32)

    gk = FAST_SLABS * kh
    n_fast = n_kh // FAST_SLABS

    def fast_scores(n, p_ref):
        off = pl.multiple_of(n * gk, gk)
        s = jnp.dot(k_ref[pl.ds(off, gk), :], qx_ref[...], preferred_element_type=F32)
        p_ref[...] = jnp.exp2(s).astype(BF16)

    def fast_values(n, p_ref):
        vT = jnp.concatenate([vT_ref[FAST_SLABS * n + j] for j in range(FAST_SLABS)], axis=1)
        acc_ref[...] = acc_ref[...] + jnp.dot(vT, p_ref[...], preferred_element_type=F32)

    @pl.when(safe)
    def _():
        fast_scores(0, p0_ref)

        def quad(n0, last):
            fast_scores(n0 + 1, p1_ref)
            fast_values(n0, p0_ref)
            fast_scores(n0 + 2, p0_ref)
            fast_values(n0 + 1, p1_ref)
            fast_scores(n0 + 3, p1_ref)
            fast_values(n0 + 2, p0_ref)
            if not last:
                fast_scores(n0 + 4, p0_ref)
            fast_values(n0 + 3, p1_ref)

        def body(j, carry):
            quad(4 * j, False)
            return carry

        lax.fori_loop(0, n_fast // 4 - 1, body, 0)
        quad(n_fast - 4, True)

    @pl.when(jnp.logical_not(safe))
    def _():
        _flash_running_max(k_ref, qT_ref, vT_ref, acc_ref, m_ref, s0_ref, s1_ref, mc0_ref, mc1_ref,
                           p0_ref, p1_ref, al0_ref, al1_ref, tk=tk, n_chunks=n_chunks, tq=tq)

    acc = acc_ref[...]
    o_ref[...] = (acc * (1.0 / acc[A_V:A_V + 1, :])).T.astype(o_ref.dtype)


def _flash_running_max(k_ref, qT_ref, vT_ref, acc_ref, m_ref, s0_ref, s1_ref, mc0_ref, mc1_ref,
                       p0_ref, p1_ref, al0_ref, al1_ref, *, tk, n_chunks, tq):
    rg = FLASH_ROW_GROUP
    n_groups = tk // rg
    m_ref[...] = jnp.full(m_ref.shape, NEG_BIG, F32)

    def stage_a(c, g, s_ref, mc_ref):
        off = pl.multiple_of(c * tk + g * rg, rg)
        sg = jnp.dot(k_ref[pl.ds(off, rg), :], qT_ref[...], preferred_element_type=F32)
        s_ref[g * rg:(g + 1) * rg, :] = sg
        part = jnp.max(sg.reshape(rg // 8, 8, tq), axis=0)
        mc_ref[...] = part if g == 0 else jnp.maximum(mc_ref[...], part)

    def stage_b_head(mc_ref, al_ref):
        m_prev = m_ref[...]
        m_new = jnp.maximum(m_prev, jnp.max(mc_ref[...], axis=0, keepdims=True))
        al_ref[...] = jnp.exp2(m_prev - m_new)
        m_ref[...] = m_new

    def stage_b(g, s_ref, p_ref):
        rows = slice(g * rg, (g + 1) * rg)
        p_ref[rows, :] = jnp.exp2(s_ref[rows, :] - m_ref[...]).astype(BF16)

    def stage_c(c, p_ref, al_ref):
        pv = jnp.dot(vT_ref[c], p_ref[0:tk, :], preferred_element_type=F32)
        acc_ref[...] = acc_ref[...] * al_ref[...] + pv

    def step(c_a, c_c, s_in, mc_in, s_out, mc_out, p_in, al_in, p_out, al_out):
        stage_b_head(mc_in, al_out)
        stage_c(c_c, p_in, al_in)
        for g in range(n_groups):
            stage_a(c_a, g, s_out, mc_out)
            stage_b(g, s_in, p_out)

    for g in range(n_groups):
        stage_a(0, g, s0_ref, mc0_ref)
    stage_b_head(mc0_ref, al0_ref)
    for g in range(n_groups):
        stage_a(1, g, s1_ref, mc1_ref)
        stage_b(g, s0_ref, p0_ref)

    def pair(j, carry):
        c0 = 2 * j
        step(jnp.minimum(c0 + 2, n_chunks - 1), c0, s1_ref, mc1_ref, s0_ref, mc0_ref, p0_ref, al0_ref, p1_ref, al1_ref)
        step(jnp.minimum(c0 + 3, n_chunks - 1), c0 + 1, s0_ref, mc0_ref, s1_ref, mc1_ref, p1_ref, al1_ref, p0_ref, al0_ref)
        return carry

    lax.fori_loop(0, n_chunks // 2, pair, 0)


def _flash(q, k, vT, *, rep, tq, tk, name):
    nb, hq = q.shape[:2]
    seq = k.shape[2]
    assert vT.shape[2:] == (seq // tk, LANES, tk) and q.shape[2:] == (seq // tk, LANES, tk)
    n_chunks = seq // tk
    kh = tk
    assert n_chunks % (4 * FAST_SLABS) == 0
    body = functools.partial(_flash_body, tk=tk, n_chunks=n_chunks, rep=rep)
    return pl.pallas_call(
        body,
        grid=(nb, hq, seq // tq),
        in_specs=[pl.BlockSpec((None, None, tq // tk, LANES, tk), lambda b, h, i: (b, h, i, 0, 0)),
                  pl.BlockSpec((None, None, seq, LANES), lambda b, h, i: (b, h // rep, 0, 0)),
                  pl.BlockSpec((None, None, seq // kh, LANES, kh), lambda b, h, i: (b, h // rep, 0, 0, 0))],
        out_specs=pl.BlockSpec((None, None, tq, LANES), lambda b, h, i: (b, h, i, 0)),
        out_shape=jax.ShapeDtypeStruct((nb, hq, seq, LANES), BF16),
        scratch_shapes=[pltpu.VMEM((LANES, tq), BF16), pltpu.VMEM((LANES, tq), BF16),
                        pltpu.VMEM((1, tq), F32),
                        pltpu.VMEM((LANES, tq), F32), pltpu.VMEM((1, tq), F32),
                        pltpu.VMEM((tk, tq), F32), pltpu.VMEM((tk, tq), F32),
                        pltpu.VMEM((8, tq), F32), pltpu.VMEM((8, tq), F32),
                        pltpu.VMEM((FAST_SLABS * tk, tq), BF16), pltpu.VMEM((FAST_SLABS * tk, tq), BF16),
                        pltpu.VMEM((1, tq), F32), pltpu.VMEM((1, tq), F32)],
        compiler_params=_params("arbitrary", "arbitrary", "arbitrary"),
        name=name,
    )(q, k, vT)


B_TU = 128
B_WIN = B_TU + 2 * B_SIDE
B_SUB = 4


def _dilated_body(q_ref, kp_ref, kc_ref, kn_ref, vp_ref, vc_ref, vn_ref, bias_ref, o_ref, lse_ref, *, n_u):
    u0 = pl.program_id(2) * (B_SUB * B_TU)
    kall = jnp.concatenate([kp_ref[B_TU - B_SIDE:, :], kc_ref[...], kn_ref[:B_SIDE, :]], axis=0)
    vall = jnp.concatenate([vp_ref[B_TU - B_SIDE:, :], vc_ref[...], vn_ref[:B_SIDE, :]], axis=0)
    col = lax.broadcasted_iota(jnp.int32, (2 * B_TU, B_WIN), 1)
    lane = lax.broadcasted_iota(jnp.int32, (B_TU, LANES), 1)
    low = lane < B_HD
    n_pairs = B_HEADS // 2
    lanes = [slice(hp * LANES, (hp + 1) * LANES) for hp in range(n_pairs)]
    units = [(t, hp) for t in range(B_SUB) for hp in range(n_pairs)]
    scores = []
    for t, hp in units:
        key_u = u0 + t * B_TU - B_SIDE + col
        valid = (key_u >= 0) & (key_u < n_u)
        qp = q_ref[t * B_TU:(t + 1) * B_TU, lanes[hp]]
        zero = jnp.zeros_like(qp)
        q2 = jnp.concatenate([jnp.where(low, qp, zero), jnp.where(low, zero, qp)], axis=0)
        kw = kall[t * B_TU:t * B_TU + B_WIN, lanes[hp]]
        s = lax.dot_general(q2, kw, (((1,), (1,)), ((), ())), preferred_element_type=F32)
        scores.append(jnp.where(valid, s + bias_ref[hp], NEG_BIG))
    maxes = [jnp.max(s, axis=-1, keepdims=True) for s in scores]
    probs = [jnp.exp2(s - m) for s, m in zip(scores, maxes)]
    sums = [jnp.sum(p, axis=-1, keepdims=True) for p in probs]
    outs = [jnp.dot(p.astype(BF16), vall[t * B_TU:t * B_TU + B_WIN, lanes[hp]], preferred_element_type=F32) * (1.0 / l)
            for (t, hp), p, l in zip(units, probs, sums)]
    for n, (t, hp) in enumerate(units):
        rows = slice(t * B_TU, (t + 1) * B_TU)
        lse = jnp.broadcast_to(maxes[n] + jnp.log2(sums[n]), (2 * B_TU, LANES))
        o_ref[rows, lanes[hp]] = jnp.where(low, outs[n][:B_TU], outs[n][B_TU:]).astype(o_ref.dtype)
        lse_ref[rows, lanes[hp]] = jnp.where(low, lse[:B_TU], lse[B_TU:])


def _dilated(zg, bias, *, name):
    nb, dilation, n_u, _ = zg.shape
    tile = B_SUB * B_TU
    nstep = n_u // tile
    nblk = n_u // B_TU
    hw = B_HEADS * B_HD

    def main(part):
        return pl.BlockSpec((None, None, tile, hw), lambda b, r, i: (b, r, i, part))

    def halo(part, before):
        def imap(b, r, i):
            blk = i * B_SUB - 1 if before else (i + 1) * B_SUB
            return (b, r, jnp.clip(blk, 0, nblk - 1), part)
        return pl.BlockSpec((None, None, B_TU, hw), imap)

    out_spec = pl.BlockSpec((None, None, tile, hw), lambda b, r, i: (b, r, i, 0))
    return pl.pallas_call(
        functools.partial(_dilated_body, n_u=n_u),
        grid=(nb, dilation, nstep),
        in_specs=[main(0), halo(1, True), main(1), halo(1, False), halo(2, True), main(2), halo(2, False),
                  pl.BlockSpec(bias.shape, lambda b, r, i: (0, 0, 0))],
        out_specs=[out_spec, out_spec],
        out_shape=[jax.ShapeDtypeStruct((nb, dilation, n_u, hw), BF16),
                   jax.ShapeDtypeStruct((nb, dilation, n_u, hw), F32)],
        compiler_params=_params("arbitrary", "arbitrary", "arbitrary"),
        name=name,
    )(zg, zg, zg, zg, zg, zg, zg, bias)


def _merge_body(x_ref, oa_ref, oc_ref, ob0_ref, ob1_ref, ob2_ref, l0_ref, l1_ref, l2_ref, gmix_ref, wg_ref,
                wa_ref, wb_ref, wc_ref, wo_ref, out_ref, on_ref, ln_ref):
    tm = x_ref.shape[0]
    x = x_ref[...]
    gate = jnp.dot(_rms(x, gmix_ref[...]).astype(BF16), wg_ref[...], preferred_element_type=F32)
    n_lt = on_ref.shape[1]
    for g, (o_ref, l_ref) in enumerate(((ob0_ref, l0_ref), (ob1_ref, l1_ref), (ob2_ref, l2_ref))):
        d = B_PAIRS[g][1]
        for r in range(d):
            for c in range(n_lt):
                cols = slice(c * LANES, (c + 1) * LANES)
                on_ref[g, c, pl.ds(r, tm // d, stride=d), :] = o_ref[r, :, cols].astype(F32)
                ln_ref[g, c, pl.ds(r, tm // d, stride=d), :] = l_ref[r, :, cols]
    wide = lambda ref, g: jnp.concatenate([ref[g, c] for c in range(n_lt)], axis=1)
    l0, l1, l2 = wide(ln_ref, 0), wide(ln_ref, 1), wide(ln_ref, 2)
    mx = jnp.maximum(jnp.maximum(l0, l1), l2)
    e0, e1, e2 = jnp.exp2(l0 - mx), jnp.exp2(l1 - mx), jnp.exp2(l2 - mx)
    ob = e0 * wide(on_ref, 0) + e1 * wide(on_ref, 1) + e2 * wide(on_ref, 2)
    ob = (ob * (1.0 / (e0 + e1 + e2))).astype(BF16)
    sg = 1.0 / (1.0 + jnp.exp(-gate))
    heads_to_lanes = lambda ref: jnp.concatenate([ref[h] for h in range(ref.shape[0])], axis=1)
    mix = sg[:, 0:D_MODEL] * jnp.dot(heads_to_lanes(oa_ref), wa_ref[...], preferred_element_type=F32)
    mix += sg[:, D_MODEL:2 * D_MODEL] * jnp.dot(ob, wb_ref[...], preferred_element_type=F32)
    mix += sg[:, 2 * D_MODEL:] * jnp.dot(heads_to_lanes(oc_ref), wc_ref[...], preferred_element_type=F32)
    out_ref[...] = x + jnp.dot(mix.astype(BF16), wo_ref[...], preferred_element_type=F32)


def _merge(x, oa, oc, obs, lses, gmix, wg, wa, wb, wc, wo, *, seq, tm):
    rows = x.shape[0]
    tiles_per_seq = seq // tm
    row_spec = lambda w: pl.BlockSpec((tm, w), lambda i: (i, 0))
    full = _resident
    hw = B_HEADS * B_HD
    res_specs = [pl.BlockSpec((None, d, tm // d, hw), lambda i: (i // tiles_per_seq, 0, i % tiles_per_seq, 0))
                 for _, d in B_PAIRS]
    head_spec = lambda o: pl.BlockSpec((None, o.shape[1], tm, LANES),
                                       lambda i: (i // tiles_per_seq, 0, i % tiles_per_seq, 0))
    return pl.pallas_call(
        _merge_body,
        grid=(rows // tm,),
        in_specs=[row_spec(D_MODEL), head_spec(oa), head_spec(oc)] + res_specs + res_specs
                 + [full(gmix), full(wg), full(wa), full(wb), full(wc), full(wo)],
        out_specs=row_spec(D_MODEL),
        out_shape=jax.ShapeDtypeStruct((rows, D_MODEL), F32),
        scratch_shapes=[pltpu.VMEM((B_GROUPS, hw // LANES, tm, LANES), F32),
                        pltpu.VMEM((B_GROUPS, hw // LANES, tm, LANES), F32)],
        compiler_params=_params("arbitrary"),
        name="merge_out",
    )(x, oa, oc, *obs, *lses, gmix, wg, wa, wb, wc, wo)


FFN_CHUNK = 1024


def _ffn_body(x_ref, g_ref, wup_ref, wdn_ref, gfin_ref, out_ref, *, final):
    x = x_ref[...]
    h = _rms(x, g_ref[...]).astype(BF16)
    acc = x
    for c in range(D_FF // FFN_CHUNK):
        cs = slice(c * FFN_CHUNK, (c + 1) * FFN_CHUNK)
        u = jnp.dot(h, wup_ref[:, cs], preferred_element_type=F32)
        a = jnp.square(jnp.maximum(u, 0.0)).astype(BF16)
        acc = acc + jnp.dot(a, wdn_ref[cs, :], preferred_element_type=F32)
    if final:
        acc = _rms(acc, gfin_ref[...])
    out_ref[...] = acc


def _ffn(x, g, wup, wdn, gfin, *, final, tm):
    rows = x.shape[0]
    full = _resident
    return pl.pallas_call(
        functools.partial(_ffn_body, final=final),
        grid=(rows // tm,),
        in_specs=[pl.BlockSpec((tm, D_MODEL), lambda i: (i, 0)), full(g), full(wup), full(wdn), full(gfin)],
        out_specs=pl.BlockSpec((tm, D_MODEL), lambda i: (i, 0)),
        out_shape=jax.ShapeDtypeStruct((rows, D_MODEL), F32),
        compiler_params=_params("arbitrary"),
        name="ffn_final" if final else "ffn",
    )(x, g, wup, wdn, gfin)


def _pad_heads(w, n_heads, hd, lane_off=0):
    k = w.shape[0]
    w = w.reshape(k, n_heads, hd)
    w = jnp.pad(w, ((0, 0), (0, 0), (lane_off, LANES - lane_off - hd)))
    return w.reshape(k, n_heads * LANES)


def _pad_head_rows(w, n_heads, hd):
    n = w.shape[1]
    return jnp.pad(w.reshape(n_heads, hd, n), ((0, 0), (0, LANES - hd), (0, 0))).reshape(n_heads * LANES, n)


def _rot_cols(w):
    half = w.shape[-1] // 2
    return jnp.concatenate([-w[..., half:], w[..., :half]], axis=-1)


def _rot_axial(w):
    half = C_HD // 2
    return jnp.concatenate([_rot_cols(w[..., :half]), _rot_cols(w[..., half:])], axis=-1)


def _swap_axial(g):
    q = C_HD // 4
    return jnp.concatenate([g[q:2 * q], g[:q], g[3 * q:], g[2 * q:3 * q]], axis=-1)


def _rope_angles(pos, half):
    freqs = ROPE_THETA ** (-jnp.arange(half, dtype=F32) / half)
    return pos.astype(F32)[:, None] * freqs[None, :]


def _tables(seq):
    pos = jnp.arange(seq, dtype=jnp.int32)
    ang = _rope_angles(pos, A_ROPE // 2)
    cos_r = jnp.tile(jnp.cos(ang), (1, 2))
    sin_r = jnp.tile(jnp.sin(ang), (1, 2))
    zeros = lambda w: jnp.zeros((seq, w), F32)
    cos_k = jnp.concatenate([zeros(A_NOPE), cos_r, zeros(LANES - A_NOPE - A_ROPE)], axis=1)
    sin_k = jnp.concatenate([zeros(A_NOPE), sin_r, zeros(LANES - A_NOPE - A_ROPE)], axis=1)
    scale_a = (A_NOPE + A_ROPE) ** -0.5 * LOG2_E
    nope = jnp.concatenate([jnp.ones((seq, A_NOPE), F32), zeros(LANES - A_NOPE)], axis=1)
    cos_q = (nope + cos_k) * scale_a
    sin_q = sin_k * scale_a
    quarter = C_HD // 4
    ang_r = _rope_angles(pos // GRID_W, quarter)
    ang_c = _rope_angles(pos % GRID_W, quarter)
    cos_c = jnp.concatenate([jnp.tile(jnp.cos(ang_r), (1, 2)), jnp.tile(jnp.cos(ang_c), (1, 2)), zeros(LANES - C_HD)], axis=1)
    sin_c = jnp.concatenate([jnp.tile(jnp.sin(ang_r), (1, 2)), jnp.tile(jnp.sin(ang_c), (1, 2)), zeros(LANES - C_HD)], axis=1)
    return jnp.stack([cos_q, sin_q, cos_k, sin_k, cos_c, sin_c], axis=0)


def _t5_bucket(rel):
    half = NUM_BUCKETS // 2
    max_exact = half // 2
    ret = jnp.where(rel > 0, half, 0)
    n = jnp.abs(rel)
    nf = jnp.maximum(n, 1).astype(F32)
    large = max_exact + (jnp.log(nf / max_exact) / math.log(MAX_DISTANCE / max_exact) * (half - max_exact)).astype(jnp.int32)
    large = jnp.minimum(large, half - 1)
    return ret + jnp.where(n < max_exact, n, large)


def _band_bias(t5_table, group, dilation):
    offs = dilation * jnp.arange(-B_SIDE, B_SIDE + 1, dtype=jnp.int32)
    bias_hj = t5_table[_t5_bucket(offs)][:, group * B_HEADS:(group + 1) * B_HEADS].T.astype(F32) * LOG2_E
    period = B_WIN + B_TU + 1
    neg = jnp.full((B_HEADS, B_TU), NEG_BIG, F32)
    padded = jnp.concatenate([neg, bias_hj, neg], axis=1)
    flow = jnp.tile(padded, (1, B_TU + 1))[:, :B_TU * (period - 1)].reshape(B_HEADS, B_TU, period - 1)
    tiles = flow[:, :, B_TU:]
    return tiles.reshape(B_HEADS // 2, 2 * B_TU, B_WIN)


def _layer_weights(l, norm_mix, w_in, a_q_norm, a_kv_norm, a_w_uq, a_w_ukv, c_q_norm, c_k_norm,
                   w_br_a, w_br_b, w_br_c, w_out, norm_ffn, w_up, w_down):
    w = w_in[l]
    o = 0
    cols = []
    for width in (A_Q_LORA, A_KV_LORA, A_ROPE, 3 * B_GROUPS * B_HEADS * B_HD, C_HEADS * C_HD,
                  C_KV_HEADS * C_HD, C_KV_HEADS * C_HD, N_BRANCH * D_MODEL):
        cols.append(w[:, o:o + width])
        o += width
    w_cq, w_ckv, w_kr, w_b, w_qc, w_kc, w_vc, w_gate = cols
    kr_p = _pad_heads(w_kr, 1, A_ROPE, A_NOPE)
    kr_rot_p = _pad_heads(_rot_cols(w_kr), 1, A_ROPE, A_NOPE)
    qc = w_qc.reshape(D_MODEL, C_HEADS, C_HD)
    kc = w_kc.reshape(D_MODEL, C_KV_HEADS, C_HD)
    w1 = jnp.concatenate([
        w_cq, w_ckv, kr_p, kr_rot_p,
        _pad_heads(w_qc, C_HEADS, C_HD), _pad_heads(_rot_axial(qc).reshape(D_MODEL, -1), C_HEADS, C_HD),
        _pad_heads(w_kc, C_KV_HEADS, C_HD), _pad_heads(_rot_axial(kc).reshape(D_MODEL, -1), C_KV_HEADS, C_HD),
        _pad_heads(w_vc, C_KV_HEADS, C_HD)], axis=1).astype(BF16)
    uq = a_w_uq[l].reshape(A_Q_LORA, A_HEADS, A_NOPE + A_ROPE)
    uq_rot = jnp.concatenate([jnp.zeros((A_Q_LORA, A_HEADS, A_NOPE), F32), _rot_cols(uq[..., A_NOPE:])], axis=-1)
    wq2 = jnp.concatenate([_pad_heads(uq.reshape(A_Q_LORA, -1), A_HEADS, A_NOPE + A_ROPE),
                           _pad_heads(uq_rot.reshape(A_Q_LORA, -1), A_HEADS, A_NOPE + A_ROPE)], axis=1).astype(BF16)
    ukv = a_w_ukv[l].reshape(A_KV_LORA, A_HEADS, A_NOPE + A_V)
    wkv2 = jnp.concatenate([_pad_heads(ukv[..., :A_NOPE].reshape(A_KV_LORA, -1), A_HEADS, A_NOPE),
                            _pad_heads(ukv[..., A_NOPE:].reshape(A_KV_LORA, -1), A_HEADS, A_V)], axis=1).astype(BF16)
    pad_gain = lambda g: jnp.pad(g, (0, LANES - C_HD))
    scale_c = C_HD ** -0.5 * LOG2_E
    cg = jnp.stack([pad_gain(c_q_norm[l]) * scale_c, pad_gain(_swap_axial(c_q_norm[l])) * scale_c,
                    pad_gain(c_k_norm[l]), pad_gain(_swap_axial(c_k_norm[l]))], axis=0)
    wb5 = w_b.reshape(D_MODEL, 3, B_GROUPS, B_HEADS * B_HD)
    wb5 = wb5 * jnp.array([B_HD ** -0.5 * LOG2_E, 1.0, 1.0], F32)[None, :, None, None]
    w_bq = jnp.transpose(wb5, (0, 2, 1, 3)).reshape(D_MODEL, -1).astype(BF16)
    return dict(
        gmix=norm_mix[l][None, :], w1=w1, gq=a_q_norm[l][None, :], gkv=a_kv_norm[l][None, :], wq2=wq2, wkv2=wkv2,
        cg=cg, w_bq=w_bq, w_gate=w_gate.astype(BF16),
        wa=_pad_head_rows(w_br_a[l], A_HEADS, A_V).astype(BF16), wb=w_br_b[l].astype(BF16),
        wc=_pad_head_rows(w_br_c[l], C_HEADS, C_HD).astype(BF16), wo=w_out[l].astype(BF16),
        gffn=norm_ffn[l][None, :], wup=w_up[l].astype(BF16), wdn=w_down[l].astype(BF16))


def _encoder_layer(x, wts, tabs, biases, *, nb, seq, final, final_norm, tm, tq, tk):
    qa, ka, va, qc, kc, vc = _prep_ac(x, wts["gmix"], wts["w1"], wts["gq"], wts["gkv"], wts["wq2"], wts["wkv2"],
                                      wts["cg"], tabs, seq=seq, tm=tm)
    zbs = _proj_b(x, wts["gmix"], wts["w_bq"], nb=nb, seq=seq, tm=tm)

    assert tm == tk
    oa = _flash(qa, ka, va, rep=1, tq=tq, tk=tk, name="flash_a")
    oc = _flash(qc, kc, vc, rep=C_HEADS // C_KV_HEADS, tq=tq, tk=tk, name="flash_c")

    obs, lses = [], []
    for g in range(B_GROUPS):
        o_g, lse_g = _dilated(zbs[g], biases[g], name=f"dilated_{g}")
        obs.append(o_g)
        lses.append(lse_g)

    x = _merge(x, oa, oc, obs, lses, wts["gmix"], wts["w_gate"], wts["wa"], wts["wb"], wts["wc"], wts["wo"],
               seq=seq, tm=tm)
    return _ffn(x, wts["gffn"], wts["wup"], wts["wdn"], final_norm, final=final, tm=tm)


def _trunk(x, norm_mix, w_in, a_q_norm, a_kv_norm, a_w_uq, a_w_ukv, c_q_norm, c_k_norm,
           w_br_a, w_br_b, w_br_c, w_out, norm_ffn, w_up, w_down, t5_table, final_norm, *, tm, tq, tk):
    nb, seq, _ = x.shape
    tabs = _tables(seq)
    biases = [_band_bias(t5_table, g, d) for g, (_, d) in enumerate(B_PAIRS)]
    depth = w_in.shape[0]
    xr = x.reshape(nb * seq, D_MODEL)
    for l in range(depth):
        wts = _layer_weights(l, norm_mix, w_in, a_q_norm, a_kv_norm, a_w_uq, a_w_ukv, c_q_norm, c_k_norm,
                             w_br_a, w_br_b, w_br_c, w_out, norm_ffn, w_up, w_down)
        xr = _encoder_layer(xr, wts, tabs, biases, nb=nb, seq=seq, final=(l == depth - 1),
                            final_norm=final_norm[None, :], tm=tm, tq=tq, tk=tk)
    return xr.reshape(nb, seq, D_MODEL)


def kernel(x_prompt, x_sample, norm_mix, w_in, a_q_norm, a_kv_norm, a_w_uq, a_w_ukv, c_q_norm, c_k_norm,
           w_br_a, w_br_b, w_br_c, w_out, norm_ffn, w_up, w_down, t5_table, final_norm):
    assert x_prompt.shape[1:] == x_sample.shape[1:]
    n_prompt = x_prompt.shape[0]
    x = jnp.concatenate([x_prompt, x_sample], axis=0)
    y = _trunk(x, norm_mix, w_in, a_q_norm, a_kv_norm, a_w_uq, a_w_ukv, c_q_norm, c_k_norm,
               w_br_a, w_br_b, w_br_c, w_out, norm_ffn, w_up, w_down, t5_table, final_norm,
               tm=512, tq=2048, tk=512)
    return (y[:n_prompt], y[n_prompt:])
```
